```python
import math
import jax, jax.numpy as jnp
from jax import lax
import numpy as np

D_MODEL = 1024
BATCH = 4
SEQ = 8192
DEPTH = 1

CHUNK = 64
D_MIX = D_MODEL
M_HEADS = 4
M_WIDTH = D_MIX // 2
M_HEAD_DIM = M_WIDTH // M_HEADS
CONV_K = 4
A_HEADS = 4
A_WIDTH = D_MIX - M_WIDTH
A_VDIM = A_WIDTH // A_HEADS
A_QKDIM = A_VDIM // 2
D_IN = 4 * M_WIDTH + 2 * M_HEADS + 3 * A_WIDTH
N_BUCKETS = 32
MAX_DISTANCE = 128
Q_BLOCK = 128
N_EXPERTS = 32
TOP_K = 4
D_FF = D_MODEL
SWIGLU_LIMIT = 7.0
SWIGLU_ALPHA = 1.702
EXPERT_BLOCK = 128
EPS = 1e-5

kernel_name = "hybrid_mlstm_diffattn_moe_block"


def rms_norm(x, g):
    xf = x.astype(jnp.float32)
    y = xf * lax.rsqrt(jnp.mean(xf * xf, axis=-1, keepdims=True) + EPS)
    return (y * g.astype(jnp.float32)).astype(x.dtype)


def t5_buckets(rel):
    nb = N_BUCKETS // 2
    max_exact = nb // 2
    ret = (rel > 0).astype(jnp.int32) * nb
    n = jnp.abs(rel)
    nf = jnp.maximum(n, 1).astype(jnp.float32)
    large = max_exact + (jnp.log(nf / max_exact) / math.log(MAX_DISTANCE / max_exact)
                         * (nb - max_exact)).astype(jnp.int32)
    large = jnp.minimum(large, nb - 1)
    return ret + jnp.where(n < max_exact, n, large)


def causal_conv(u, w, b):
    c = u.shape[-1]
    y = lax.conv_general_dilated(u, w[:, None, :], window_strides=(1,),
                                 padding=[(CONV_K - 1, 0)],
                                 dimension_numbers=('NWC', 'WIO', 'NWC'),
                                 feature_group_count=c)
    return y + b


def mlstm_chunkwise(q, k, v, i_pre, f_pre):
    b_, s_, h_, d_ = q.shape
    nc = s_ // CHUNK
    f32 = jnp.float32

    def chunks(t):
        return t.astype(f32).reshape(b_, nc, CHUNK, h_, d_).transpose(1, 0, 3, 2, 4)

    def gchunks(t):
        return t.astype(f32).reshape(b_, nc, CHUNK, h_).transpose(1, 0, 3, 2)

    qc, vc = chunks(q), chunks(v)
    kc = chunks(k) * (d_ ** -0.5)
    log_i = gchunks(i_pre)
    log_f = jax.nn.log_sigmoid(gchunks(f_pre))
    tri = jnp.tril(jnp.ones((CHUNK, CHUNK), bool))

    def step(carry, inp):
        C, n, m = carry
        qb, kb, vb, li, lf = inp
        bcum = jnp.cumsum(lf, axis=-1)
        dmat = jnp.where(tri, bcum[..., :, None] - bcum[..., None, :] + li[..., None, :], -jnp.inf)
        m_inter = bcum + m[..., None]
        m_t = jnp.maximum(jnp.max(dmat, axis=-1), m_inter)
        w = jnp.exp(dmat - m_t[..., None]) * jnp.einsum('bhtd,bhsd->bhts', qb, kb)
        sc = jnp.exp(m_inter - m_t)
        num = jnp.einsum('bhts,bhse->bhte', w, vb) + sc[..., None] * jnp.einsum('bhtd,bhde->bhte', qb, C)
        den = jnp.sum(w, axis=-1) + sc * jnp.einsum('bhtd,bhd->bht', qb, n)
        h_out = num / jnp.maximum(jnp.abs(den), jnp.exp(-m_t))[..., None]
        b_last = bcum[..., -1]
        g = b_last[..., None] - bcum + li
        m_new = jnp.maximum(b_last + m, jnp.max(g, axis=-1))
        wk = jnp.exp(g - m_new[..., None])[..., None] * kb
        decay = jnp.exp(b_last + m - m_new)
        C_new = decay[..., None, None] * C + jnp.einsum('bhsd,bhse->bhde', wk, vb)
        n_new = decay[..., None] * n + jnp.sum(wk, axis=-2)
        return (C_new, n_new, m_new), h_out

    init = (jnp.zeros((b_, h_, d_, d_), f32), jnp.zeros((b_, h_, d_), f32), jnp.zeros((b_, h_), f32))
    _, hs = lax.scan(step, init, (qc, kc, vc, log_i, log_f))
    return hs.transpose(1, 0, 3, 2, 4).reshape(b_, s_, h_, d_)


def diff_attention(q, k, v, rel_bias, lam):
    b_, s_ = q.shape[:2]
    scale = A_QKDIM ** -0.5
    k_pos = jnp.arange(s_)
    k_chunk = k_pos // CHUNK
    vf = v.astype(jnp.float32)
    table = rel_bias.astype(jnp.float32)

    def block(qi):
        start = qi * Q_BLOCK
        qb = lax.dynamic_slice_in_dim(q, start, Q_BLOCK, axis=1)
        q_pos = start + jnp.arange(Q_BLOCK)
        bias = jnp.transpose(table[t5_buckets(k_pos[None, :] - q_pos[:, None])], (2, 0, 1))
        s = jnp.einsum('bqhmd,bkhmd->bhmqk', qb, k).astype(jnp.float32) * scale + bias[None, :, None]
        mask = k_chunk[None, :] <= (q_pos // CHUNK)[:, None]
        p = jax.nn.softmax(jnp.where(mask, s, -jnp.inf), axis=-1)
        pd = p[:, :, 0] - lam * p[:, :, 1]
        return jnp.einsum('bhqk,bkhe->bqhe', pd, vf)

    out = lax.map(block, jnp.arange(s_ // Q_BLOCK))
    return out.transpose(1, 0, 2, 3, 4).reshape(b_, s_, A_HEADS, A_VDIM)


def moe(u, router_w, router_b, w_gu, b_gu, w_down, b_down):
    t_ = u.shape[0]
    tk = t_ * TOP_K
    logits = (u @ router_w + router_b).astype(jnp.float32)
    top_vals, top_idx = lax.top_k(logits, TOP_K)
    gates = jax.nn.softmax(top_vals, axis=-1).astype(u.dtype)
    flat_e = top_idx.reshape(-1)
    flat_tok = jnp.arange(tk, dtype=jnp.int32) // TOP_K
    flat_g = gates.reshape(-1)
    order = jnp.argsort(flat_e)
    sorted_e = flat_e[order]
    counts = jnp.bincount(flat_e, length=N_EXPERTS)
    padded = ((counts + EXPERT_BLOCK - 1) // EXPERT_BLOCK) * EXPERT_BLOCK
    starts = jnp.cumsum(counts) - counts
    pends = jnp.cumsum(padded)
    pstarts = pends - padded
    dest = pstarts[sorted_e] + jnp.arange(tk) - starts[sorted_e]
    p_rows = tk + N_EXPERTS * EXPERT_BLOCK
    n_blocks = p_rows // EXPERT_BLOCK
    buf_tok = jnp.zeros((p_rows,), jnp.int32).at[dest].set(flat_tok[order])
    buf_g = jnp.zeros((p_rows,), u.dtype).at[dest].set(flat_g[order])
    block_e = jnp.minimum(jnp.searchsorted(pends, jnp.arange(n_blocks) * EXPERT_BLOCK, side='right'),
                          N_EXPERTS - 1)

    def expert_block(args):
        tok, e = args
        xb = u[tok]
        gu = xb @ w_gu[e] + b_gu[e]
        gate = jnp.minimum(gu[:, :D_FF], SWIGLU_LIMIT)
        up = jnp.clip(gu[:, D_FF:], -SWIGLU_LIMIT, SWIGLU_LIMIT)
        act = (up + 1) * gate * jax.nn.sigmoid(SWIGLU_ALPHA * gate)
        return act @ w_down[e] + b_down[e]

    out = lax.map(expert_block, (buf_tok.reshape(n_blocks, EXPERT_BLOCK), block_e))
    out = out.reshape(p_rows, -1) * buf_g[:, None]
    return jnp.zeros_like(u).at[buf_tok].add(out)


def setup_inputs(seed: int = 0) -> dict:
    key = jax.random.key(seed)
    ks = jax.random.split(key, 24)
    f32 = jnp.float32
    nrm = lambda k, s, sc: jax.random.normal(k, s, f32) * sc
    gate_b = jnp.concatenate([
        nrm(ks[5], (DEPTH, M_HEADS), 0.1),
        3.0 + 3.0 * jax.random.uniform(ks[6], (DEPTH, M_HEADS), f32)], axis=-1)
    return {
        "x": jax.random.normal(ks[0], (BATCH, SEQ, D_MODEL), f32),
        "rel_bias": nrm(ks[1], (N_BUCKETS, A_HEADS), 0.5),
        "norm1_g": 1.0 + nrm(ks[2], (DEPTH, D_MODEL), 0.02),
        "w_in": nrm(ks[3], (DEPTH, D_MODEL, D_IN), D_MODEL ** -0.5),
        "conv_w": nrm(ks[4], (DEPTH, CONV_K, 2 * M_WIDTH), CONV_K ** -0.5),
        "conv_b": nrm(ks[7], (DEPTH, 2 * M_WIDTH), 0.01),
        "gate_b": gate_b,
        "mlstm_norm_g": 1.0 + nrm(ks[8], (DEPTH, M_WIDTH), 0.02),
        "lam_q1": nrm(ks[9], (DEPTH, A_QKDIM), 0.1),
        "lam_k1": nrm(ks[10], (DEPTH, A_QKDIM), 0.1),
        "lam_q2": nrm(ks[11], (DEPTH, A_QKDIM), 0.1),
        "lam_k2": nrm(ks[12], (DEPTH, A_QKDIM), 0.1),
        "diff_norm_g": 1.0 + nrm(ks[13], (DEPTH, A_VDIM), 0.02),
        "w_out": nrm(ks[14], (DEPTH, D_MIX, D_MODEL), D_MIX ** -0.5),
        "norm2_g": 1.0 + nrm(ks[15], (DEPTH, D_MODEL), 0.02),
        "router_w": nrm(ks[16], (DEPTH, D_MODEL, N_EXPERTS), D_MODEL ** -0.5),
        "router_b": nrm(ks[17], (DEPTH, N_EXPERTS), 0.01),
        "w_gu": nrm(ks[18], (DEPTH, N_EXPERTS, D_MODEL, 2 * D_FF), D_MODEL ** -0.5),
        "b_gu": nrm(ks[19], (DEPTH, N_EXPERTS, 2 * D_FF), 0.01),
        "w_down": nrm(ks[20], (DEPTH, N_EXPERTS, D_FF, D_MODEL), D_FF ** -0.5),
        "b_down": nrm(ks[21], (DEPTH, N_EXPERTS, D_MODEL), 0.01),
        "final_g": 1.0 + nrm(ks[22], (D_MODEL,), 0.02),
    }


def reference(x, rel_bias, norm1_g, w_in, conv_w, conv_b, gate_b, mlstm_norm_g,
              lam_q1, lam_k1, lam_q2, lam_k2, diff_norm_g, w_out, norm2_g,
              router_w, router_b, w_gu, b_gu, w_down, b_down, final_g):
    b_, s_, _ = x.shape
    f32 = jnp.float32
    h = x
    for l in range(DEPTH):
        u = rms_norm(h, norm1_g[l])
        proj = u @ w_in[l]
        c0 = 4 * M_WIDTH
        c1 = c0 + 2 * M_HEADS
        mqk = jax.nn.silu(causal_conv(proj[..., :2 * M_WIDTH], conv_w[l], conv_b[l]))
        mq = mqk[..., :M_WIDTH].reshape(b_, s_, M_HEADS, M_HEAD_DIM)
        mk = mqk[..., M_WIDTH:].reshape(b_, s_, M_HEADS, M_HEAD_DIM)
        mv = proj[..., 2 * M_WIDTH:3 * M_WIDTH].reshape(b_, s_, M_HEADS, M_HEAD_DIM)
        mo = proj[..., 3 * M_WIDTH:c0]
        mg = proj[..., c0:c1].astype(f32) + gate_b[l].astype(f32)
        m_h = mlstm_chunkwise(mq, mk, mv, mg[..., :M_HEADS], mg[..., M_HEADS:])
        m_h = rms_norm(m_h, mlstm_norm_g[l].reshape(M_HEADS, M_HEAD_DIM))
        m_out = (jax.nn.sigmoid(mo.astype(f32)) * m_h.reshape(b_, s_, M_WIDTH)).astype(x.dtype)
        a = proj[..., c1:]
        aq = a[..., :A_WIDTH].reshape(b_, s_, A_HEADS, 2, A_QKDIM)
        ak = a[..., A_WIDTH:2 * A_WIDTH].reshape(b_, s_, A_HEADS, 2, A_QKDIM)
        av = a[..., 2 * A_WIDTH:].reshape(b_, s_, A_HEADS, A_VDIM)
        lam_init = 0.8 - 0.6 * math.exp(-0.3 * l)
        lam = (jnp.exp(jnp.sum(lam_q1[l].astype(f32) * lam_k1[l].astype(f32)))
               - jnp.exp(jnp.sum(lam_q2[l].astype(f32) * lam_k2[l].astype(f32))) + lam_init)
        a_h = diff_attention(aq, ak, av, rel_bias, lam)
        a_out = (rms_norm(a_h, diff_norm_g[l]) * (1.0 - lam_init)).reshape(b_, s_, A_WIDTH).astype(x.dtype)
        h = h + jnp.concatenate([m_out, a_out], axis=-1) @ w_out[l]
        u2 = rms_norm(h, norm2_g[l]).reshape(b_ * s_, D_MODEL)
        h = h + moe(u2, router_w[l], router_b[l], w_gu[l], b_gu[l], w_down[l], b_down[l]).reshape(b_, s_, D_MODEL)
    return rms_norm(h, final_g)
```

```python
import functools
import math

import jax
import jax.numpy as jnp
from jax import lax
from jax.experimental import pallas as pl
from jax.experimental.pallas import tpu as pltpu

F32 = jnp.float32
BF16 = jnp.bfloat16
EPS = 1e-5
NEG = -1e30

CHUNK = 64
M_HEADS = 4
A_HEADS = 4
CONV_K = 4
N_BUCKETS = 32
MAX_DISTANCE = 128
TOP_K = 4
SWIGLU_LIMIT = 7.0
SWIGLU_ALPHA = 1.702

LANES = 128
ATT_BLOCK = 256
EXPERT_ROWS = 256
VMEM_LIMIT = 56 * 1024 * 1024


def _cparams(sem):
    return pltpu.CompilerParams(dimension_semantics=sem, vmem_limit_bytes=VMEM_LIMIT)


def _dot(a, b):
    return jnp.dot(a, b, preferred_element_type=F32)


def _dot_nt(a, b):
    return lax.dot_general(a, b, (((1,), (1,)), ((), ())), preferred_element_type=F32)


def _dot_tn(a, b):
    return lax.dot_general(a, b, (((0,), (0,)), ((), ())), preferred_element_type=F32)


def _log_sigmoid(x):
    return jnp.minimum(x, 0.0) - jnp.log(1.0 + jnp.exp(-jnp.abs(x)))


def _sigmoid(x):
    return 1.0 / (1.0 + jnp.exp(-x))


def _inproj_body(x_ref, g_ref, wa_ref, wt_ref, gbr_ref, gbc_ref,
                 mqk_ref, mv_ref, mo_ref, ak_ref, gc_ref, qt_ref, vt_ref, gt_ref,
                 *, mw, aw, tm):
    x = x_ref[...]
    u = x * lax.rsqrt(jnp.mean(x * x, axis=-1, keepdims=True) + EPS) * g_ref[...]
    ub = u.astype(BF16)

    def mm(a, b):
        return _dot(ub, wa_ref[:, a:b])

    c0 = 2 * mw
    mqk_ref[...] = mm(0, c0).astype(BF16)
    mv_ref[...] = mm(c0, c0 + mw).astype(BF16)
    mo_ref[...] = mm(c0 + mw, c0 + 2 * mw).astype(BF16)
    c1 = c0 + 2 * mw
    ak_ref[...] = mm(c1, c1 + aw).astype(BF16)
    gc_ref[...] = mm(c1 + aw, c1 + aw + 2 * LANES) + gbr_ref[...]
    t = _dot_nt(wt_ref[...], ub)
    qt_ref[...] = t[0:aw].astype(BF16)
    for j in range(tm // ATT_BLOCK):
        vt_ref[j] = t[aw:2 * aw, j * ATT_BLOCK:(j + 1) * ATT_BLOCK].astype(BF16)
    gt_ref[...] = t[2 * aw:2 * aw + 16] + gbc_ref[...]


def _inproj(x2, g1, wa, wt, gbr, gbc, *, mw, aw, tm):
    t_, d = x2.shape
    nt = t_ // tm
    body = functools.partial(_inproj_body, mw=mw, aw=aw, tm=tm)
    row = lambda w: pl.BlockSpec((tm, w), lambda i: (i, 0))
    full = lambda a: pl.BlockSpec(a.shape, lambda i: (0,) * a.ndim)
    return pl.pallas_call(
        body,
        grid=(nt,),
        in_specs=[row(d), full(g1), full(wa), full(wt), full(gbr), full(gbc)],
        out_specs=[row(2 * mw), row(mw), row(mw), row(aw), row(2 * LANES),
                   pl.BlockSpec((aw, tm), lambda i: (0, i)),
                   pl.BlockSpec((tm // ATT_BLOCK, aw, ATT_BLOCK), lambda i: (i, 0, 0)),
                   pl.BlockSpec((16, tm), lambda i: (0, i))],
        out_shape=[jax.ShapeDtypeStruct((t_, 2 * mw), BF16),
                   jax.ShapeDtypeStruct((t_, mw), BF16),
                   jax.ShapeDtypeStruct((t_, mw), BF16),
                   jax.ShapeDtypeStruct((t_, aw), BF16),
                   jax.ShapeDtypeStruct((t_, 2 * LANES), F32),
                   jax.ShapeDtypeStruct((aw, t_), BF16),
                   jax.ShapeDtypeStruct((t_ // ATT_BLOCK, aw, ATT_BLOCK), BF16),
                   jax.ShapeDtypeStruct((16, t_), F32)],
        compiler_params=_cparams(("arbitrary",)),
        name="inproj",
    )(x2, g1, wa, wt, gbr, gbc)


def _segment_scan(x, pos, axis, op, identity):
    d = 1
    while d < CHUNK:
        sh = pltpu.roll(x, d, axis=axis)
        x = op(x, jnp.where(pos >= d, sh, identity))
        d *= 2
    return x


def _mlstm_body(mqk_ref, mv_ref, mo_ref, gc_ref, gt_ref, cw_ref, cb_ref, ng_ref, out_ref,
                ext_ref, qk_ref, gs_ref, cst_ref, mst_ref, *, tb, nh, dh):
    mw = nh * dh

    @pl.when(pl.program_id(1) == 0)
    def _():
        ext_ref[0:8, :] = jnp.zeros((8, 2 * mw), F32)
        cst_ref[...] = jnp.zeros_like(cst_ref)
        mst_ref[...] = jnp.zeros_like(mst_ref)

    ext_ref[8:8 + tb, :] = mqk_ref[...].astype(F32)
    y = cb_ref[...]
    for j in range(CONV_K):
        y = y + cw_ref[j:j + 1, :] * ext_ref[8 - (CONV_K - 1) + j:8 - (CONV_K - 1) + j + tb, :]
    ext_ref[0:8, :] = ext_ref[tb:tb + 8, :]
    act = y * _sigmoid(y)
    qk_ref[:, 0:mw] = act[:, 0:mw].astype(BF16)
    qk_ref[:, mw:2 * mw] = (act[:, mw:2 * mw] * (dh ** -0.5)).astype(BF16)

    gc = gc_ref[...]
    rpos = lax.broadcasted_iota(jnp.int32, (tb, LANES), 0) & (CHUNK - 1)
    bc = _segment_scan(_log_sigmoid(gc[:, LANES:]), rpos, 0, jnp.add, 0.0)
    a_c = gc[:, :LANES] - bc
    pm = _segment_scan(a_c, rpos, 0, jnp.maximum, NEG)
    gs_ref[0] = bc
    gs_ref[1] = a_c
    gs_ref[2] = pm
    gt = gt_ref[...]
    lpos = lax.broadcasted_iota(jnp.int32, (8, tb), 1) & (CHUNK - 1)
    a_r = gt[0:8] - _segment_scan(_log_sigmoid(gt[8:16]), lpos, 1, jnp.add, 0.0)

    tri = (lax.broadcasted_iota(jnp.int32, (CHUNK, CHUNK), 0)
           >= lax.broadcasted_iota(jnp.int32, (CHUNK, CHUNK), 1))
    ones_col = (lax.broadcasted_iota(jnp.int32, (CHUNK, dh), 1) == 0).astype(BF16)

    for c in range(tb // CHUNK):
        r0 = c * CHUNK
        for h in range(nh):
            q = qk_ref[r0:r0 + CHUNK, h * dh:(h + 1) * dh]
            k = qk_ref[r0:r0 + CHUNK, mw + h * dh:mw + (h + 1) * dh]
            vext = jnp.concatenate([mv_ref[r0:r0 + CHUNK, h * dh:(h + 1) * dh], ones_col], axis=1)
            m_prev = mst_ref[h:h + 1, 0:1]
            bc_h = gs_ref[0, r0:r0 + CHUNK, h:h + 1]
            ac_h = gs_ref[1, r0:r0 + CHUNK, h:h + 1]
            pm_h = gs_ref[2, r0:r0 + CHUNK, h:h + 1]
            big_m = jnp.maximum(pm_h, m_prev)
            e = jnp.where(tri, jnp.exp(a_r[h:h + 1, r0:r0 + CHUNK] - big_m), 0.0)
            w = (e * _dot_nt(q, k)).astype(BF16)
            cext = cst_ref[h]
            r = _dot(w, vext) + jnp.exp(m_prev - big_m) * _dot(q, cext.astype(BF16))
            num = r[:, 0:dh]
            den = r[:, dh:dh + 1]
            hh = num / jnp.maximum(jnp.abs(den), jnp.exp(-(bc_h + big_m)))
            yn = hh * lax.rsqrt(jnp.mean(hh * hh, axis=-1, keepdims=True) + EPS) * ng_ref[h:h + 1, :]
            og = _sigmoid(mo_ref[r0:r0 + CHUNK, h * dh:(h + 1) * dh].astype(F32))
            out_ref[r0:r0 + CHUNK, h * dh:(h + 1) * dh] = (og * yn).astype(BF16)
            m_last = jnp.maximum(m_prev, pm_h[CHUNK - 1:CHUNK, :])
            wk = (jnp.exp(ac_h - m_last) * k.astype(F32)).astype(BF16)
            cst_ref[h] = jnp.exp(m_prev - m_last) * cext + _dot_tn(wk, vext)
            mst_ref[h:h + 1, :] = jnp.broadcast_to(bc_h[CHUNK - 1:CHUNK, :] + m_last, (1, LANES))


def _mlstm(mqk, mv, mo, gc, gt, cw, cb, ng, *, batch, seq, nh, dh, tb):
    mw = nh * dh
    nb = seq // tb
    body = functools.partial(_mlstm_body, tb=tb, nh=nh, dh=dh)
    row = lambda w: pl.BlockSpec((tb, w), lambda b, j: (b * nb + j, 0))
    full = lambda a: pl.BlockSpec(a.shape, lambda b, j: (0,) * a.ndim)
    return pl.pallas_call(
        body,
        grid=(batch, nb),
        in_specs=[row(2 * mw), row(mw), row(mw), row(2 * LANES),
                  pl.BlockSpec((16, tb), lambda b, j: (0, b * nb + j)),
                  full(cw), full(cb), full(ng)],
        out_specs=row(mw),
        out_shape=jax.ShapeDtypeStruct((batch * seq, mw), BF16),
        scratch_shapes=[pltpu.VMEM((tb + 8, 2 * mw), F32),
                        pltpu.VMEM((tb, 2 * mw), BF16),
                        pltpu.VMEM((3, tb, LANES), F32),
                        pltpu.VMEM((nh, dh, 2 * dh), F32),
                        pltpu.VMEM((8, LANES), F32)],
        compiler_params=_cparams(("arbitrary", "arbitrary")),
        name="mlstm",
    )(mqk, mv, mo, gc, gt, cw, cb, ng)


def _attn_body(qt_ref, k_ref, vt_ref, bias_ref, lam_ref, ng_ref, out_ref, acc_ref, ml_ref,
               *, lam_init):
    bq = ATT_BLOCK
    qi = pl.program_id(2)
    qt = qt_ref[...]
    half = qt.shape[0] // 2
    rowi = lax.broadcasted_iota(jnp.int32, qt.shape, 0)
    zero = jnp.zeros_like(qt)
    qs = (jnp.where(rowi < half, qt, zero), jnp.where(rowi >= half, qt, zero))
    acc_ref[...] = jnp.zeros_like(acc_ref)
    ml_ref[...] = jnp.where(lax.broadcasted_iota(jnp.int32, ml_ref.shape, 0) % 2 == 0, NEG, 0.0)

    def step(kj, bias):
        kb = k_ref[pl.ds(pl.multiple_of(kj * bq, bq), bq), :]
        vb = vt_ref[kj]
        for m in range(2):
            s = _dot(kb, qs[m])
            if bias is not None:
                s = s + bias
            m_old = ml_ref[2 * m:2 * m + 1, :]
            m_new = jnp.maximum(m_old, jnp.max(s, axis=0, keepdims=True))
            alpha = jnp.exp(m_old - m_new)
            p = jnp.exp(s - m_new)
            ml_ref[2 * m:2 * m + 1, :] = m_new
            ml_ref[2 * m + 1:2 * m + 2, :] = alpha * ml_ref[2 * m + 1:2 * m + 2, :] + jnp.sum(p, axis=0, keepdims=True)
            acc_ref[m] = alpha * acc_ref[m] + _dot(vb, p.astype(BF16))

    def plain(kj, carry):
        step(kj, None)
        return carry

    lax.fori_loop(0, jnp.maximum(qi - 1, 0), plain, 0)

    @pl.when(qi >= 1)
    def _():
        step(qi - 1, bias_ref[0, 0])

    step(qi, bias_ref[0, 1])

    lv = lam_ref[...]
    lam = (jnp.exp(jnp.sum(lv[0:1] * lv[1:2], axis=-1, keepdims=True))
           - jnp.exp(jnp.sum(lv[2:3] * lv[3:4], axis=-1, keepdims=True)) + lam_init)
    o = acc_ref[0] / ml_ref[1:2, :] - lam * (acc_ref[1] / ml_ref[3:4, :])
    yn = o * lax.rsqrt(jnp.mean(o * o, axis=0, keepdims=True) + EPS) * ng_ref[...] * (1.0 - lam_init)
    out_ref[...] = yn.T.astype(BF16)


def _attention(qt, ak, vt, bias, lamv, ngb, *, batch, seq, nh, dv, lam_init):
    bq = ATT_BLOCK
    nq = seq // bq
    body = functools.partial(_attn_body, lam_init=lam_init)
    return pl.pallas_call(
        body,
        grid=(batch, nh, nq),
        in_specs=[pl.BlockSpec((dv, bq), lambda b, h, i: (h, b * nq + i)),
                  pl.BlockSpec((seq, dv), lambda b, h, i: (b, h)),
                  pl.BlockSpec((nq, dv, bq), lambda b, h, i: (b, h, 0)),
                  pl.BlockSpec((1, 2, bq, bq), lambda b, h, i: (h, 0, 0, 0)),
                  pl.BlockSpec(lamv.shape, lambda b, h, i: (0, 0)),
                  pl.BlockSpec(ngb.shape, lambda b, h, i: (0, 0))],
        out_specs=pl.BlockSpec((bq, dv), lambda b, h, i: (b * nq + i, h)),
        out_shape=jax.ShapeDtypeStruct((batch * seq, nh * dv), BF16),
        scratch_shapes=[pltpu.VMEM((2, dv, bq), F32), pltpu.VMEM((8, bq), F32)],
        compiler_params=_cparams(("arbitrary", "arbitrary", "arbitrary")),
        name="diffattn",
    )(qt, ak, vt, bias, lamv, ngb)


def _rel_bucket(rel):
    nb = N_BUCKETS // 2
    max_exact = nb // 2
    ret = (rel > 0).astype(jnp.int32) * nb
    n = jnp.abs(rel)
    nf = jnp.maximum(n, 1).astype(F32)
    large = max_exact + (jnp.log(nf / max_exact) / math.log(MAX_DISTANCE / max_exact)
                         * (nb - max_exact)).astype(jnp.int32)
    large = jnp.minimum(large, nb - 1)
    return ret + jnp.where(n < max_exact, n, large)


def _bias_tiles(rel_bias):
    b = ATT_BLOCK
    table = rel_bias.astype(F32)
    kpos = jnp.arange(b)[:, None]
    qpos = jnp.arange(b)[None, :]
    far = table[_rel_bucket(jnp.array(-(2 * b), jnp.int32))]
    prev = table[_rel_bucket(kpos - b - qpos)] - far
    diag = table[_rel_bucket(kpos - qpos)] - far
    diag = jnp.where((kpos // CHUNK <= qpos // CHUNK)[..., None], diag, NEG)
    return jnp.transpose(jnp.stack([prev, diag], axis=0), (3, 0, 1, 2))


def _router_body(x_ref, mo_ref, ao_ref, wo_ref, g2_ref, wr_ref, br_ref, tril_ref,
                 h1_ref, u2_ref, idx_ref, gate_ref, rank_ref, cnt_ref, run_ref, *, mw):
    @pl.when(pl.program_id(0) == 0)
    def _():
        run_ref[...] = jnp.zeros_like(run_ref)

    h1 = x_ref[...] + _dot(mo_ref[...], wo_ref[0:mw, :]) + _dot(ao_ref[...], wo_ref[mw:, :])
    h1_ref[...] = h1
    u2 = h1 * lax.rsqrt(jnp.mean(h1 * h1, axis=-1, keepdims=True) + EPS) * g2_ref[...]
    u2_ref[...] = u2
    logits = _dot(u2.astype(BF16), wr_ref[...]) + br_ref[...]
    lane = lax.broadcasted_iota(jnp.int32, logits.shape, 1).astype(F32)
    l = logits
    vals, sels, idxs = [], [], []
    for _ in range(TOP_K):
        mx = jnp.max(l, axis=-1, keepdims=True)
        ik = jnp.min(jnp.where(l == mx, lane, float(LANES)), axis=-1, keepdims=True)
        sel = lane == ik
        l = jnp.where(sel, -jnp.inf, l)
        vals.append(mx)
        idxs.append(ik)
        sels.append(sel)
    es = [jnp.exp(v - vals[0]) for v in vals]
    inv = 1.0 / sum(es)
    oh = sum(s.astype(F32) for s in sels)
    tot = _dot(tril_ref[...], oh.astype(BF16)) + run_ref[...]
    ranks = [jnp.sum(jnp.where(s, tot, 0.0), axis=-1, keepdims=True) for s in sels]
    run_ref[...] = run_ref[...] + jnp.sum(oh, axis=0, keepdims=True)
    cnt_ref[...] = run_ref[...]

    def pack(cols):
        out = jnp.zeros(logits.shape, F32)
        for j, cj in enumerate(cols):
            out = jnp.where(lane == float(j), cj, out)
        return out

    idx_ref[...] = pack(idxs).astype(jnp.int32)
    gate_ref[...] = pack([e * inv for e in es])
    rank_ref[...] = pack(ranks).astype(jnp.int32)


def _router(x2, m_out, a_out, wo, g2, wr, br, *, tm):
    t_, d = x2.shape
    mw = m_out.shape[1]
    nt = t_ // tm
    tril = (jnp.arange(tm)[:, None] > jnp.arange(tm)[None, :]).astype(BF16)
    body = functools.partial(_router_body, mw=mw)
    row = lambda w: pl.BlockSpec((tm, w), lambda i: (i, 0))
    full = lambda a: pl.BlockSpec(a.shape, lambda i: (0,) * a.ndim)
    return pl.pallas_call(
        body,
        grid=(nt,),
        in_specs=[row(d), row(mw), row(a_out.shape[1]), full(wo), full(g2), full(wr), full(br), full(tril)],
        out_specs=[row(d), row(d), row(LANES), row(LANES), row(LANES),
                   pl.BlockSpec((1, LANES), lambda i: (0, 0))],
        out_shape=[jax.ShapeDtypeStruct((t_, d), F32),
                   jax.ShapeDtypeStruct((t_, d), F32),
                   jax.ShapeDtypeStruct((t_, LANES), jnp.int32),
                   jax.ShapeDtypeStruct((t_, LANES), F32),
                   jax.ShapeDtypeStruct((t_, LANES), jnp.int32),
                   jax.ShapeDtypeStruct((1, LANES), F32)],
        scratch_shapes=[pltpu.VMEM((1, LANES), F32)],
        compiler_params=_cparams(("arbitrary",)),
        name="router",
    )(x2, m_out, a_out, wo, g2, wr, br, tril)


def _dispatch_body(pos_ref, u2_ref, xs_in_ref, xs_ref, sem, *, tmd):
    del xs_in_ref
    base = pl.program_id(0) * tmd
    n = tmd * TOP_K

    def copy(p):
        return pltpu.make_async_copy(u2_ref.at[pl.ds(base + p // TOP_K, 1)],
                                     xs_ref.at[pl.ds(pos_ref[0, 0, p], 1)], sem)

    def start(p, c):
        copy(p).start()
        return c

    def wait(p, c):
        copy(p).wait()
        return c

    lax.fori_loop(0, n, start, 0)
    lax.fori_loop(0, n, wait, 0)


def _dispatch(pos, u2, xs0, *, tmd):
    t_ = u2.shape[0]
    nt = t_ // tmd
    pos3 = pos.reshape(nt, 1, tmd * TOP_K)
    return pl.pallas_call(
        functools.partial(_dispatch_body, tmd=tmd),
        grid=(nt,),
        in_specs=[pl.BlockSpec((1, 1, tmd * TOP_K), lambda i: (i, 0, 0), memory_space=pltpu.SMEM),
                  pl.BlockSpec(memory_space=pl.ANY),
                  pl.BlockSpec(memory_space=pl.ANY)],
        out_specs=pl.BlockSpec(memory_space=pl.ANY),
        out_shape=jax.ShapeDtypeStruct(xs0.shape, xs0.dtype),
        scratch_shapes=[pltpu.SemaphoreType.DMA],
        input_output_aliases={2: 0},
        compiler_params=_cparams(("arbitrary",)),
        name="moe_dispatch",
    )(pos3, u2, xs0)


def _experts_body(be_ref, nu_ref, xs_ref, wgu_ref, bgu_ref, wd_ref, bd_ref, ys_ref, *, dff):
    i = pl.program_id(0)

    @pl.when(i < nu_ref[0])
    def _():
        gu = _dot(xs_ref[...].astype(BF16), wgu_ref[0]) + bgu_ref[0]
        gate = jnp.minimum(gu[:, :dff], SWIGLU_LIMIT)
        up = jnp.clip(gu[:, dff:], -SWIGLU_LIMIT, SWIGLU_LIMIT)
        act = (up + 1.0) * gate * _sigmoid(SWIGLU_ALPHA * gate)
        ys_ref[...] = _dot(act.astype(BF16), wd_ref[0]) + bd_ref[0]

    @pl.when(i >= nu_ref[0])
    def _():
        ys_ref[...] = jnp.zeros_like(ys_ref)


def _experts(blk_e, n_used, xs, wgu, bgu, wd, bd):
    p_rows, d = xs.shape
    r = EXPERT_ROWS
    dff = wd.shape[1]
    grid_spec = pltpu.PrefetchScalarGridSpec(
        num_scalar_prefetch=2,
        grid=(p_rows // r,),
        in_specs=[pl.BlockSpec((r, d), lambda i, be, nu: (i, 0)),
                  pl.BlockSpec((1, d, 2 * dff), lambda i, be, nu: (be[i], 0, 0)),
                  pl.BlockSpec((1, 1, 2 * dff), lambda i, be, nu: (be[i], 0, 0)),
                  pl.BlockSpec((1, dff, d), lambda i, be, nu: (be[i], 0, 0)),
                  pl.BlockSpec((1, 1, d), lambda i, be, nu: (be[i], 0, 0))],
        out_specs=pl.BlockSpec((r, d), lambda i, be, nu: (i, 0)),
    )
    return pl.pallas_call(
        functools.partial(_experts_body, dff=dff),
        grid_spec=grid_spec,
        out_shape=jax.ShapeDtypeStruct((p_rows, d), F32),
        compiler_params=_cparams(("arbitrary",)),
        name="moe_experts",
    )(blk_e, n_used, xs, wgu, bgu, wd, bd)


def _combine_body(pos_ref, h1_ref, gate_ref, fg_ref, ys_ref, out_ref, rows_ref, sem, *, tmc, final):
    n = tmc * TOP_K

    def copy(p):
        return pltpu.make_async_copy(ys_ref.at[pl.ds(pos_ref[0, 0, p], 1)],
                                     rows_ref.at[p % TOP_K, pl.ds(p // TOP_K, 1)], sem)

    def start(p, c):
        copy(p).start()
        return c

    def wait(p, c):
        copy(p).wait()
        return c

    lax.fori_loop(0, n, start, 0)
    lax.fori_loop(0, n, wait, 0)
    g = gate_ref[...]
    h = h1_ref[...]
    for k in range(TOP_K):
        h = h + g[:, k:k + 1] * rows_ref[k]
    if final:
        h = h * lax.rsqrt(jnp.mean(h * h, axis=-1, keepdims=True) + EPS) * fg_ref[...]
    out_ref[...] = h


def _combine(pos, h1, gates, fg, ys, *, tmc, final):
    t_, d = h1.shape
    nt = t_ // tmc
    pos3 = pos.reshape(nt, 1, tmc * TOP_K)
    row = lambda w: pl.BlockSpec((tmc, w), lambda i: (i, 0))
    return pl.pallas_call(
        functools.partial(_combine_body, tmc=tmc, final=final),
        grid=(nt,),
        in_specs=[pl.BlockSpec((1, 1, tmc * TOP_K), lambda i: (i, 0, 0), memory_space=pltpu.SMEM),
                  row(d), row(LANES), pl.BlockSpec(fg.shape, lambda i: (0, 0)),
                  pl.BlockSpec(memory_space=pl.ANY)],
        out_specs=row(d),
        out_shape=jax.ShapeDtypeStruct((t_, d), F32),
        scratch_shapes=[pltpu.VMEM((TOP_K, tmc, d), F32), pltpu.SemaphoreType.DMA],
        compiler_params=_cparams(("arbitrary",)),
        name="moe_combine",
    )(pos3, h1, gates, fg, ys)


def _pick_tile(n, pref):
    t = pref
    while n % t:
        t //= 2
    return t


def kernel(x, rel_bias, norm1_g, w_in, conv_w, conv_b, gate_b, mlstm_norm_g, lam_q1, lam_k1, lam_q2,
           lam_k2, diff_norm_g, w_out, norm2_g, router_w, router_b, w_gu, b_gu, w_down, b_down, final_g):
    batch, seq, d = x.shape
    depth = w_in.shape[0]
    t_ = batch * seq
    nh = M_HEADS
    mw = conv_w.shape[-1] // 2
    dh = mw // nh
    aw = w_out.shape[1] - mw
    dv = aw // A_HEADS
    n_exp = router_w.shape[-1]
    dff = w_down.shape[-2]
    assert seq % ATT_BLOCK == 0 and dh == LANES and dv == LANES and n_exp <= LANES
    tm = _pick_tile(t_, 512)
    tb = ATT_BLOCK
    r = EXPERT_ROWS
    p_rows = t_ * TOP_K + n_exp * r

    bias_tiles = _bias_tiles(rel_bias)
    h = x.reshape(t_, d)
    for l in range(depth):
        w = w_in[l]
        c0 = 4 * mw
        c1 = c0 + 2 * nh
        zc = jnp.zeros((d, LANES - nh), F32)
        wa = jnp.concatenate([w[:, 0:2 * mw], w[:, 2 * mw:3 * mw], w[:, 3 * mw:c0],
                              w[:, c1 + aw:c1 + 2 * aw],
                              w[:, c0:c0 + nh], zc, w[:, c0 + nh:c1], zc], axis=1).astype(BF16)
        z4 = jnp.zeros((d, 8 - nh), F32)
        wt = jnp.concatenate([w[:, c1:c1 + aw] * (dv // 2) ** -0.5, w[:, c1 + 2 * aw:c1 + 3 * aw],
                              w[:, c0:c0 + nh], z4, w[:, c0 + nh:c1], z4], axis=1).T.astype(BF16)
        gb = gate_b[l].astype(F32)
        zl = jnp.zeros((LANES - nh,), F32)
        gbr = jnp.concatenate([gb[:nh], zl, gb[nh:], zl])[None, :]
        z8 = jnp.zeros((8 - nh,), F32)
        gbc = jnp.broadcast_to(jnp.concatenate([gb[:nh], z8, gb[nh:], z8])[:, None], (16, tm))
        mqk, mv, mo, ak, gc, qt, vt, gt = _inproj(h, norm1_g[l][None, :], wa, wt, gbr, gbc,
                                                  mw=mw, aw=aw, tm=tm)
        m_out = _mlstm(mqk, mv, mo, gc, gt, conv_w[l], conv_b[l][None, :],
                       mlstm_norm_g[l].reshape(nh, dh), batch=batch, seq=seq, nh=nh, dh=dh, tb=tb)
        lam_init = 0.8 - 0.6 * math.exp(-0.3 * l)
        lamv = jnp.stack([lam_q1[l], lam_k1[l], lam_q2[l], lam_k2[l]]).astype(F32)
        ngb = jnp.broadcast_to(diff_norm_g[l].astype(F32)[:, None], (dv, ATT_BLOCK))
        a_out = _attention(qt, ak, vt, bias_tiles, lamv, ngb, batch=batch, seq=seq, nh=A_HEADS, dv=dv, lam_init=lam_init)
        wr = jnp.concatenate([router_w[l], jnp.zeros((d, LANES - n_exp), F32)], axis=1).astype(BF16)
        br = jnp.concatenate([router_b[l].astype(F32), jnp.full((LANES - n_exp,), NEG, F32)])[None, :]
        h1, u2, idx, gates, rank, cnt = _router(h, m_out, a_out, w_out[l].astype(BF16), norm2_g[l][None, :],
                                                wr, br, tm=tm)
        counts = cnt[0, :n_exp].astype(jnp.int32)
        padded = ((counts + r - 1) // r) * r
        pends = jnp.cumsum(padded)
        pstarts = pends - padded
        pos = (pstarts[idx[:, :TOP_K]] + rank[:, :TOP_K]).reshape(-1)
        blk_e = jnp.minimum(jnp.searchsorted(pends, jnp.arange(p_rows // r) * r, side='right'),
                            n_exp - 1).astype(jnp.int32)
        n_used = (pends[-1:] // r).astype(jnp.int32)
        xs = _dispatch(pos, u2, jnp.zeros((p_rows, d), F32), tmd=_pick_tile(t_, 256))
        ys = _experts(blk_e, n_used, xs, w_gu[l].astype(BF16), b_gu[l][:, None, :].astype(F32),
                      w_down[l].astype(BF16), b_down[l][:, None, :].astype(F32))
        h = _combine(pos, h1, gates, final_g[None, :].astype(F32), ys, tmc=_pick_tile(t_, 64),
                     final=(l == depth - 1))
    return h.reshape(batch, seq, d)
```

```python
import functools
import math

import jax
import jax.numpy as jnp
from jax import lax
from jax.experimental import pallas as pl
from jax.experimental.pallas import tpu as pltpu

F32 = jnp.float32
BF16 = jnp.bfloat16
EPS = 1e-5
NEG = -1e30
LOG2E = math.log2(math.e)

CHUNK = 64
M_HEADS = 4
A_HEADS = 4
CONV_K = 4
N_BUCKETS = 32
MAX_DISTANCE = 128
TOP_K = 4
SWIGLU_LIMIT = 7.0
SWIGLU_ALPHA = 1.702

LANES = 128
ROW_TILE = 8
ATT_BLOCK = 256
EXPERT_ROWS = 256
VMEM_LIMIT = 56 * 1024 * 1024


def _cparams(sem):
    return pltpu.CompilerParams(dimension_semantics=sem, vmem_limit_bytes=VMEM_LIMIT)


def _dot(a, b):
    return jnp.dot(a, b, preferred_element_type=F32)


def _dot_nt(a, b):
    return lax.dot_general(a, b, (((1,), (1,)), ((), ())), preferred_element_type=F32)


def _dot_tn(a, b):
    return lax.dot_general(a, b, (((0,), (0,)), ((), ())), preferred_element_type=F32)


def _log_sigmoid(x):
    return jnp.minimum(x, 0.0) - jnp.log(1.0 + jnp.exp(-jnp.abs(x)))


def _sigmoid(x):
    return 1.0 / (1.0 + jnp.exp(-x))


def _inproj_body(x_ref, g_ref, wa_ref, wt_ref, gbr_ref, gbc_ref,
                 mqk_ref, mv_ref, mo_ref, ak_ref, gc_ref, qt_ref, vt_ref, gt_ref,
                 *, mw, aw, tm):
    x = x_ref[...]
    u = x * lax.rsqrt(jnp.mean(x * x, axis=-1, keepdims=True) + EPS) * g_ref[...]
    ub = u.astype(BF16)

    def mm(a, b):
        return _dot(ub, wa_ref[:, a:b])

    c0 = 2 * mw
    mqk_ref[...] = mm(0, c0).astype(BF16)
    mv_ref[...] = mm(c0, c0 + mw).astype(BF16)
    mo_ref[...] = mm(c0 + mw, c0 + 2 * mw).astype(BF16)
    c1 = c0 + 2 * mw
    ak_ref[...] = mm(c1, c1 + aw).astype(BF16)
    gc_ref[...] = mm(c1 + aw, c1 + aw + 2 * LANES) + gbr_ref[...]
    t = _dot_nt(wt_ref[...], ub)
    qt_ref[...] = t[0:aw].astype(BF16)
    for j in range(tm // ATT_BLOCK):
        vt_ref[j] = t[aw:2 * aw, j * ATT_BLOCK:(j + 1) * ATT_BLOCK].astype(BF16)
    gt_ref[...] = t[2 * aw:2 * aw + 16] + gbc_ref[...]


def _inproj(x2, g1, wa, wt, gbr, gbc, *, mw, aw, tm):
    t_, d = x2.shape
    nt = t_ // tm
    body = functools.partial(_inproj_body, mw=mw, aw=aw, tm=tm)
    row = lambda w: pl.BlockSpec((tm, w), lambda i: (i, 0))
    full = lambda a: pl.BlockSpec(a.shape, lambda i: (0,) * a.ndim)
    return pl.pallas_call(
        body,
        grid=(nt,),
        in_specs=[row(d), full(g1), full(wa), full(wt), full(gbr), full(gbc)],
        out_specs=[row(2 * mw), row(mw), row(mw), row(aw), row(2 * LANES),
                   pl.BlockSpec((aw, tm), lambda i: (0, i)),
                   pl.BlockSpec((tm // ATT_BLOCK, aw, ATT_BLOCK), lambda i: (i, 0, 0)),
                   pl.BlockSpec((16, tm), lambda i: (0, i))],
        out_shape=[jax.ShapeDtypeStruct((t_, 2 * mw), BF16),
                   jax.ShapeDtypeStruct((t_, mw), BF16),
                   jax.ShapeDtypeStruct((t_, mw), BF16),
                   jax.ShapeDtypeStruct((t_, aw), BF16),
                   jax.ShapeDtypeStruct((t_, 2 * LANES), F32),
                   jax.ShapeDtypeStruct((aw, t_), BF16),
                   jax.ShapeDtypeStruct((t_ // ATT_BLOCK, aw, ATT_BLOCK), BF16),
                   jax.ShapeDtypeStruct((16, t_), F32)],
        compiler_params=_cparams(("arbitrary",)),
        name="inproj",
    )(x2, g1, wa, wt, gbr, gbc)


def _segment_scan(x, pos, axis, op, identity):
    d = 1
    while d < CHUNK:
        sh = pltpu.roll(x, d, axis=axis)
        x = op(x, jnp.where(pos >= d, sh, identity))
        d *= 2
    return x


def _mlstm_body(mqk_ref, mv_ref, mo_ref, gc_ref, gt_ref, cw_ref, cb_ref, ng_ref, out_ref,
                ext_ref, qk_ref, gs_ref, cst_ref, mst_ref, *, tb, nh, dh):
    mw = nh * dh

    @pl.when(pl.program_id(1) == 0)
    def _():
        ext_ref[0:8, :] = jnp.zeros((8, 2 * mw), F32)
        cst_ref[...] = jnp.zeros_like(cst_ref)
        mst_ref[...] = jnp.zeros_like(mst_ref)

    ext_ref[8:8 + tb, :] = mqk_ref[...].astype(F32)
    y = cb_ref[...]
    for j in range(CONV_K):
        y = y + cw_ref[j:j + 1, :] * ext_ref[8 - (CONV_K - 1) + j:8 - (CONV_K - 1) + j + tb, :]
    ext_ref[0:8, :] = ext_ref[tb:tb + 8, :]
    act = y * _sigmoid(y)
    qk_ref[:, 0:mw] = act[:, 0:mw].astype(BF16)
    qk_ref[:, mw:2 * mw] = (act[:, mw:2 * mw] * (dh ** -0.5)).astype(BF16)

    gc = gc_ref[...]
    rpos = lax.broadcasted_iota(jnp.int32, (tb, LANES), 0) & (CHUNK - 1)
    bc = _segment_scan(_log_sigmoid(gc[:, LANES:]), rpos, 0, jnp.add, 0.0)
    a_c = gc[:, :LANES] - bc
    pm = _segment_scan(a_c, rpos, 0, jnp.maximum, NEG)
    gs_ref[0] = bc
    gs_ref[1] = a_c
    gs_ref[2] = pm
    gt = gt_ref[...]
    lpos = lax.broadcasted_iota(jnp.int32, (8, tb), 1) & (CHUNK - 1)
    a_r = gt[0:8] - _segment_scan(_log_sigmoid(gt[8:16]), lpos, 1, jnp.add, 0.0)

    tri = (lax.broadcasted_iota(jnp.int32, (CHUNK, CHUNK), 0)
           >= lax.broadcasted_iota(jnp.int32, (CHUNK, CHUNK), 1))
    ones_col = (lax.broadcasted_iota(jnp.int32, (CHUNK, dh), 1) == 0).astype(BF16)

    for c in range(tb // CHUNK):
        r0 = c * CHUNK
        for h in range(nh):
            q = qk_ref[r0:r0 + CHUNK, h * dh:(h + 1) * dh]
            k = qk_ref[r0:r0 + CHUNK, mw + h * dh:mw + (h + 1) * dh]
            vext = jnp.concatenate([mv_ref[r0:r0 + CHUNK, h * dh:(h + 1) * dh], ones_col], axis=1)
            m_prev = mst_ref[h:h + 1, 0:1]
            bc_h = gs_ref[0, r0:r0 + CHUNK, h:h + 1]
            ac_h = gs_ref[1, r0:r0 + CHUNK, h:h + 1]
            pm_h = gs_ref[2, r0:r0 + CHUNK, h:h + 1]
            big_m = jnp.maximum(pm_h, m_prev)
            e = jnp.where(tri, jnp.exp(a_r[h:h + 1, r0:r0 + CHUNK] - big_m), 0.0)
            w = (e * _dot_nt(q, k)).astype(BF16)
            cext = cst_ref[h]
            r = _dot(w, vext) + jnp.exp(m_prev - big_m) * _dot(q, cext.astype(BF16))
            num = r[:, 0:dh]
            den = r[:, dh:dh + 1]
            hh = num / jnp.maximum(jnp.abs(den), jnp.exp(-(bc_h + big_m)))
            yn = hh * lax.rsqrt(jnp.mean(hh * hh, axis=-1, keepdims=True) + EPS) * ng_ref[h:h + 1, :]
            og = _sigmoid(mo_ref[r0:r0 + CHUNK, h * dh:(h + 1) * dh].astype(F32))
            out_ref[r0:r0 + CHUNK, h * dh:(h + 1) * dh] = (og * yn).astype(BF16)
            m_last = jnp.maximum(m_prev, pm_h[CHUNK - 1:CHUNK, :])
            wk = (jnp.exp(ac_h - m_last) * k.astype(F32)).astype(BF16)
            cst_ref[h] = jnp.exp(m_prev - m_last) * cext + _dot_tn(wk, vext)
            mst_ref[h:h + 1, :] = jnp.broadcast_to(bc_h[CHUNK - 1:CHUNK, :] + m_last, (1, LANES))


def _mlstm(mqk, mv, mo, gc, gt, cw, cb, ng, *, batch, seq, nh, dh, tb):
    mw = nh * dh
    nb = seq // tb
    body = functools.partial(_mlstm_body, tb=tb, nh=nh, dh=dh)
    row = lambda w: pl.BlockSpec((tb, w), lambda b, j: (b * nb + j, 0))
    full = lambda a: pl.BlockSpec(a.shape, lambda b, j: (0,) * a.ndim)
    return pl.pallas_call(
        body,
        grid=(batch, nb),
        in_specs=[row(2 * mw), row(mw), row(mw), row(2 * LANES),
                  pl.BlockSpec((16, tb), lambda b, j: (0, b * nb + j)),
                  full(cw), full(cb), full(ng)],
        out_specs=row(mw),
        out_shape=jax.ShapeDtypeStruct((batch * seq, mw), BF16),
        scratch_shapes=[pltpu.VMEM((tb + 8, 2 * mw), F32),
                        pltpu.VMEM((tb, 2 * mw), BF16),
                        pltpu.VMEM((3, tb, LANES), F32),
                        pltpu.VMEM((nh, dh, 2 * dh), F32),
                        pltpu.VMEM((8, LANES), F32)],
        compiler_params=_cparams(("arbitrary", "arbitrary")),
        name="mlstm",
    )(mqk, mv, mo, gc, gt, cw, cb, ng)


def _attn_body(qt_ref, k_ref, vt_ref, bias_ref, lam_ref, ng_ref, out_ref,
               acc_ref, ml_ref, s_ref, p_ref, al_ref, *, lam_init):
    bq = ATT_BLOCK
    qi = pl.program_id(2)
    qt = qt_ref[...]
    half = qt.shape[0] // 2
    rowi = lax.broadcasted_iota(jnp.int32, qt.shape, 0)
    zero = jnp.zeros_like(qt)
    qs = (jnp.where(rowi < half, qt, zero), jnp.where(rowi >= half, qt, zero))
    acc_ref[...] = jnp.zeros_like(acc_ref)
    ml_ref[...] = jnp.where(lax.broadcasted_iota(jnp.int32, ml_ref.shape, 0) % 2 == 0, NEG, 0.0)
    p_ref[1] = jnp.zeros(p_ref.shape[1:], BF16)
    al_ref[1] = jnp.ones(al_ref.shape[1:], F32)

    def scores(j, slot):
        kb = k_ref[pl.ds(pl.multiple_of(j * bq, bq), bq), :]
        bias = bias_ref[0, jnp.clip(j - (qi - 2), 0, 2)]
        for m in range(2):
            s_ref[slot, m] = _dot(kb, qs[m]) + bias

    def softmax(slot):
        for m in range(2):
            s = s_ref[slot, m]
            m_old = ml_ref[2 * m:2 * m + 1, :]
            m_new = jnp.maximum(m_old, jnp.max(s, axis=0, keepdims=True))
            alpha = jnp.exp2(m_old - m_new)
            p = jnp.exp2(s - m_new)
            ml_ref[2 * m:2 * m + 1, :] = m_new
            ml_ref[2 * m + 1:2 * m + 2, :] = (alpha * ml_ref[2 * m + 1:2 * m + 2, :]
                                              + jnp.sum(p, axis=0, keepdims=True))
            p_ref[slot, m] = p.astype(BF16)
            al_ref[slot, m:m + 1, :] = alpha

    def values(j, slot):
        vb = vt_ref[j]
        for m in range(2):
            acc_ref[m] = al_ref[slot, m:m + 1, :] * acc_ref[m] + _dot(vb, p_ref[slot, m])

    def step(i, cur):
        scores(jnp.minimum(i + 1, qi), 1 - cur)
        softmax(cur)
        values(jnp.maximum(i - 1, 0), 1 - cur)

    def pair(t, carry):
        step(2 * t, 0)
        step(2 * t + 1, 1)
        return carry

    scores(0, 0)
    n_blocks = qi + 1
    lax.fori_loop(0, lax.shift_right_logical(n_blocks, 1), pair, 0)

    @pl.when((n_blocks & 1) == 1)
    def _():
        step(qi, 0)
        values(qi, 0)

    @pl.when((n_blocks & 1) == 0)
    def _():
        values(qi, 1)

    lv = lam_ref[...]
    lam = (jnp.exp(jnp.sum(lv[0:1] * lv[1:2], axis=-1, keepdims=True))
           - jnp.exp(jnp.sum(lv[2:3] * lv[3:4], axis=-1, keepdims=True)) + lam_init)
    o = acc_ref[0] / ml_ref[1:2, :] - lam * (acc_ref[1] / ml_ref[3:4, :])
    yn = o * lax.rsqrt(jnp.mean(o * o, axis=0, keepdims=True) + EPS) * ng_ref[...] * (1.0 - lam_init)
    out_ref[...] = yn.T.astype(BF16)


def _attention(qt, ak, vt, bias, lamv, ngb, *, batch, seq, nh, dv, lam_init):
    bq = ATT_BLOCK
    nq = seq // bq
    body = functools.partial(_attn_body, lam_init=lam_init)
    return pl.pallas_call(
        body,
        grid=(batch, nh, nq),
        in_specs=[pl.BlockSpec((dv, bq), lambda b, h, i: (h, b * nq + i)),
                  pl.BlockSpec((seq, dv), lambda b, h, i: (b, h)),
                  pl.BlockSpec((nq, dv, bq), lambda b, h, i: (b, h, 0)),
                  pl.BlockSpec((1, 3, bq, bq), lambda b, h, i: (h, 0, 0, 0)),
                  pl.BlockSpec(lamv.shape, lambda b, h, i: (0, 0)),
                  pl.BlockSpec(ngb.shape, lambda b, h, i: (0, 0))],
        out_specs=pl.BlockSpec((bq, dv), lambda b, h, i: (b * nq + i, h)),
        out_shape=jax.ShapeDtypeStruct((batch * seq, nh * dv), BF16),
        scratch_shapes=[pltpu.VMEM((2, dv, bq), F32), pltpu.VMEM((8, bq), F32),
                        pltpu.VMEM((2, 2, bq, bq), F32), pltpu.VMEM((2, 2, bq, bq), BF16),
                        pltpu.VMEM((2, 8, bq), F32)],
        compiler_params=_cparams(("arbitrary", "arbitrary", "arbitrary")),
        name="diffattn",
    )(qt, ak, vt, bias, lamv, ngb)


def _rel_bucket(rel):
    nb = N_BUCKETS // 2
    max_exact = nb // 2
    ret = (rel > 0).astype(jnp.int32) * nb
    n = jnp.abs(rel)
    nf = jnp.maximum(n, 1).astype(F32)
    large = max_exact + (jnp.log(nf / max_exact) / math.log(MAX_DISTANCE / max_exact)
                         * (nb - max_exact)).astype(jnp.int32)
    large = jnp.minimum(large, nb - 1)
    return ret + jnp.where(n < max_exact, n, large)


def _bias_tiles(rel_bias):
    b = ATT_BLOCK
    table = rel_bias.astype(F32)
    kpos = jnp.arange(b)[:, None]
    qpos = jnp.arange(b)[None, :]
    far = table[_rel_bucket(jnp.array(-(2 * b), jnp.int32))]
    prev = (table[_rel_bucket(kpos - b - qpos)] - far) * LOG2E
    diag = (table[_rel_bucket(kpos - qpos)] - far) * LOG2E
    diag = jnp.where((kpos // CHUNK <= qpos // CHUNK)[..., None], diag, NEG)
    return jnp.transpose(jnp.stack([jnp.zeros_like(prev), prev, diag], axis=0), (3, 0, 1, 2))


def _router_body(x_ref, mo_ref, ao_ref, wo_ref, g2_ref, wr_ref, br_ref, tril_ref,
                 h1_ref, u2_ref, idx_ref, gate_ref, rank_ref, cnt_ref, run_ref, *, mw):
    @pl.when(pl.program_id(0) == 0)
    def _():
        run_ref[...] = jnp.zeros_like(run_ref)

    h1 = x_ref[...] + _dot(mo_ref[...], wo_ref[0:mw, :]) + _dot(ao_ref[...], wo_ref[mw:, :])
    h1_ref[...] = h1
    u2 = h1 * lax.rsqrt(jnp.mean(h1 * h1, axis=-1, keepdims=True) + EPS) * g2_ref[...]
    for j in range(u2.shape[1] // LANES):
        u2_ref[pl.ds(j, u2.shape[0], stride=ROW_TILE), :] = u2[:, j * LANES:(j + 1) * LANES]
    logits = _dot(u2.astype(BF16), wr_ref[...]) + br_ref[...]
    lane = lax.broadcasted_iota(jnp.int32, logits.shape, 1).astype(F32)
    l = logits
    vals, sels, idxs = [], [], []
    for _ in range(TOP_K):
        mx = jnp.max(l, axis=-1, keepdims=True)
        ik = jnp.min(jnp.where(l == mx, lane, float(LANES)), axis=-1, keepdims=True)
        sel = lane == ik
        l = jnp.where(sel, -jnp.inf, l)
        vals.append(mx)
        idxs.append(ik)
        sels.append(sel)
    es = [jnp.exp(v - vals[0]) for v in vals]
    inv = 1.0 / sum(es)
    oh = sum(s.astype(F32) for s in sels)
    tot = _dot(tril_ref[...], oh.astype(BF16)) + run_ref[...]
    ranks = [jnp.sum(jnp.where(s, tot, 0.0), axis=-1, keepdims=True) for s in sels]
    run_ref[...] = run_ref[...] + jnp.sum(oh, axis=0, keepdims=True)
    cnt_ref[...] = run_ref[...]

    def pack(cols):
        out = jnp.zeros(logits.shape, F32)
        for j, cj in enumerate(cols):
            out = jnp.where(lane == float(j), cj, out)
        return out

    idx_ref[...] = pack(idxs).astype(jnp.int32)
    gate_ref[...] = pack([e * inv for e in es])
    rank_ref[...] = pack(ranks).astype(jnp.int32)


def _router(x2, m_out, a_out, wo, g2, wr, br, *, tm):
    t_, d = x2.shape
    mw = m_out.shape[1]
    nt = t_ // tm
    tril = (jnp.arange(tm)[:, None] > jnp.arange(tm)[None, :]).astype(BF16)
    body = functools.partial(_router_body, mw=mw)
    row = lambda w: pl.BlockSpec((tm, w), lambda i: (i, 0))
    full = lambda a: pl.BlockSpec(a.shape, lambda i: (0,) * a.ndim)
    return pl.pallas_call(
        body,
        grid=(nt,),
        in_specs=[row(d), row(mw), row(a_out.shape[1]), full(wo), full(g2), full(wr), full(br), full(tril)],
        out_specs=[row(d), pl.BlockSpec((tm * d // LANES, LANES), lambda i: (i, 0)),
                   row(LANES), row(LANES), row(LANES),
                   pl.BlockSpec((1, LANES), lambda i: (0, 0))],
        out_shape=[jax.ShapeDtypeStruct((t_, d), F32),
                   jax.ShapeDtypeStruct((t_ * d // LANES, LANES), F32),
                   jax.ShapeDtypeStruct((t_, LANES), jnp.int32),
                   jax.ShapeDtypeStruct((t_, LANES), F32),
                   jax.ShapeDtypeStruct((t_, LANES), jnp.int32),
                   jax.ShapeDtypeStruct((1, LANES), F32)],
        scratch_shapes=[pltpu.VMEM((1, LANES), F32)],
        compiler_params=_cparams(("arbitrary",)),
        name="router",
    )(x2, m_out, a_out, wo, g2, wr, br, tril)


def _pair_loop(n_tok, fn):
    def body(r, c):
        for k in range(TOP_K):
            fn(r, k)
        return c
    lax.fori_loop(0, n_tok, body, 0, unroll=4)


def _dispatch_body(pos_ref, u2_ref, xs_in_ref, xs_ref, sem, *, tmd):
    del xs_in_ref

    def copy(r, k):
        return pltpu.make_async_copy(u2_ref.at[r], xs_ref.at[pos_ref[0, 0, r * TOP_K + k]], sem)

    _pair_loop(tmd, lambda r, k: copy(r, k).start())
    _pair_loop(tmd, lambda r, k: copy(r, k).wait())


def _dispatch(pos, u2c, xs0, *, tmd):
    t_ = u2c.shape[0]
    nt = t_ // tmd
    pos3 = pos.reshape(nt, 1, tmd * TOP_K)
    return pl.pallas_call(
        functools.partial(_dispatch_body, tmd=tmd),
        grid=(nt,),
        in_specs=[pl.BlockSpec((1, 1, tmd * TOP_K), lambda i: (i, 0, 0), memory_space=pltpu.SMEM),
                  pl.BlockSpec((tmd,) + u2c.shape[1:], lambda i: (i, 0, 0)),
                  pl.BlockSpec(memory_space=pl.ANY)],
        out_specs=pl.BlockSpec(memory_space=pl.ANY),
        out_shape=jax.ShapeDtypeStruct(xs0.shape, xs0.dtype),
        scratch_shapes=[pltpu.SemaphoreType.DMA],
        input_output_aliases={2: 0},
        compiler_params=_cparams(("arbitrary",)),
        name="moe_dispatch",
    )(pos3, u2c, xs0)


def _experts_body(be_ref, nu_ref, xs_ref, wgu_ref, bgu_ref, wd_ref, bd_ref, ys_ref, *, dff, r):
    i = pl.program_id(0)
    nj = wd_ref.shape[2] // LANES

    @pl.when(i < nu_ref[0])
    def _():
        x = jnp.concatenate([xs_ref[pl.ds(j, r, stride=ROW_TILE), :] for j in range(nj)], axis=1)
        gu = _dot(x.astype(BF16), wgu_ref[0]) + bgu_ref[0]
        gate = jnp.minimum(gu[:, :dff], SWIGLU_LIMIT)
        up = jnp.clip(gu[:, dff:], -SWIGLU_LIMIT, SWIGLU_LIMIT)
        act = (up + 1.0) * gate * _sigmoid(SWIGLU_ALPHA * gate)
        y = _dot(act.astype(BF16), wd_ref[0]) + bd_ref[0]
        for j in range(nj):
            ys_ref[pl.ds(j, r, stride=ROW_TILE), :] = y[:, j * LANES:(j + 1) * LANES]

    @pl.when(i >= nu_ref[0])
    def _():
        ys_ref[...] = jnp.zeros_like(ys_ref)


def _experts(blk_e, n_used, xs2, wgu, bgu, wd, bd):
    r = EXPERT_ROWS
    dff, d = wd.shape[1], wd.shape[2]
    rt = r * d // LANES
    grid_spec = pltpu.PrefetchScalarGridSpec(
        num_scalar_prefetch=2,
        grid=(xs2.shape[0] // rt,),
        in_specs=[pl.BlockSpec((rt, LANES), lambda i, be, nu: (i, 0)),
                  pl.BlockSpec((1, d, 2 * dff), lambda i, be, nu: (be[i], 0, 0)),
                  pl.BlockSpec((1, 1, 2 * dff), lambda i, be, nu: (be[i], 0, 0)),
                  pl.BlockSpec((1, dff, d), lambda i, be, nu: (be[i], 0, 0)),
                  pl.BlockSpec((1, 1, d), lambda i, be, nu: (be[i], 0, 0))],
        out_specs=pl.BlockSpec((rt, LANES), lambda i, be, nu: (i, 0)),
    )
    return pl.pallas_call(
        functools.partial(_experts_body, dff=dff, r=r),
        grid_spec=grid_spec,
        out_shape=jax.ShapeDtypeStruct(xs2.shape, F32),
        compiler_params=_cparams(("arbitrary",)),
        name="moe_experts",
    )(blk_e, n_used, xs2, wgu, bgu, wd, bd)


def _combine_body(pos_ref, posn_ref, h1_ref, gate_ref, fg_ref, ys_ref, out_ref, rows_ref, sems,
                  *, tmc, final):
    i = pl.program_id(0)
    slot = i % 2

    def copy(pref, sl, r, k):
        return pltpu.make_async_copy(
            ys_ref.at[pref[0, 0, r * TOP_K + k]],
            rows_ref.at[sl, k, pl.ds(pl.multiple_of(r * ROW_TILE, ROW_TILE), ROW_TILE)],
            sems.at[sl])

    @pl.when(i == 0)
    def _():
        _pair_loop(tmc, lambda r, k: copy(pos_ref, 0, r, k).start())

    @pl.when(i + 1 < pl.num_programs(0))
    def _():
        _pair_loop(tmc, lambda r, k: copy(posn_ref, 1 - slot, r, k).start())

    _pair_loop(tmc, lambda r, k: copy(pos_ref, slot, r, k).wait())
    g = gate_ref[...]
    cols = []
    for j in range(h1_ref.shape[1] // LANES):
        hj = h1_ref[:, j * LANES:(j + 1) * LANES]
        for k in range(TOP_K):
            hj = hj + g[:, k:k + 1] * rows_ref[slot, k, pl.ds(j, tmc, stride=ROW_TILE), :]
        cols.append(hj)
    h = jnp.concatenate(cols, axis=1)
    if final:
        h = h * lax.rsqrt(jnp.mean(h * h, axis=-1, keepdims=True) + EPS) * fg_ref[...]
    out_ref[...] = h


def _combine(pos, h1, gates, fg, ys3, *, tmc, final):
    t_, d = h1.shape
    nt = t_ // tmc
    pos3 = pos.reshape(nt, 1, tmc * TOP_K)
    row = lambda w: pl.BlockSpec((tmc, w), lambda i: (i, 0))
    smem = lambda f: pl.BlockSpec((1, 1, tmc * TOP_K), f, memory_space=pltpu.SMEM)
    return pl.pallas_call(
        functools.partial(_combine_body, tmc=tmc, final=final),
        grid=(nt,),
        in_specs=[smem(lambda i: (i, 0, 0)), smem(lambda i: (jnp.minimum(i + 1, nt - 1), 0, 0)),
                  row(d), row(LANES), pl.BlockSpec(fg.shape, lambda i: (0, 0)),
                  pl.BlockSpec(memory_space=pl.ANY)],
        out_specs=row(d),
        out_shape=jax.ShapeDtypeStruct((t_, d), F32),
        scratch_shapes=[pltpu.VMEM((2, TOP_K, tmc * ROW_TILE, LANES), F32), pltpu.SemaphoreType.DMA((2,))],
        compiler_params=_cparams(("arbitrary",)),
        name="moe_combine",
    )(pos3, pos3, h1, gates, fg, ys3)


def _pick_tile(n, pref):
    t = pref
    while n % t:
        t //= 2
    return t


def kernel(x, rel_bias, norm1_g, w_in, conv_w, conv_b, gate_b, mlstm_norm_g, lam_q1, lam_k1, lam_q2,
           lam_k2, diff_norm_g, w_out, norm2_g, router_w, router_b, w_gu, b_gu, w_down, b_down, final_g):
    batch, seq, d = x.shape
    depth = w_in.shape[0]
    t_ = batch * seq
    nh = M_HEADS
    mw = conv_w.shape[-1] // 2
    dh = mw // nh
    aw = w_out.shape[1] - mw
    dv = aw // A_HEADS
    n_exp = router_w.shape[-1]
    dff = w_down.shape[-2]
    assert seq % ATT_BLOCK == 0 and dh == LANES and dv == LANES and n_exp <= LANES
    assert d == ROW_TILE * LANES, 'token rows are moved as single (8, 128) f32 tiles'
    tm = _pick_tile(t_, 512)
    tb = ATT_BLOCK
    r = EXPERT_ROWS
    p_rows = t_ * TOP_K + n_exp * r

    bias_tiles = _bias_tiles(rel_bias)
    h = x.reshape(t_, d)
    for l in range(depth):
        w = w_in[l]
        c0 = 4 * mw
        c1 = c0 + 2 * nh
        zc = jnp.zeros((d, LANES - nh), F32)
        wa = jnp.concatenate([w[:, 0:2 * mw], w[:, 2 * mw:3 * mw], w[:, 3 * mw:c0],
                              w[:, c1 + aw:c1 + 2 * aw],
                              w[:, c0:c0 + nh], zc, w[:, c0 + nh:c1], zc], axis=1).astype(BF16)
        z4 = jnp.zeros((d, 8 - nh), F32)
        wt = jnp.concatenate([w[:, c1:c1 + aw] * ((dv // 2) ** -0.5 * LOG2E), w[:, c1 + 2 * aw:c1 + 3 * aw],
                              w[:, c0:c0 + nh], z4, w[:, c0 + nh:c1], z4], axis=1).T.astype(BF16)
        gb = gate_b[l].astype(F32)
        zl = jnp.zeros((LANES - nh,), F32)
        gbr = jnp.concatenate([gb[:nh], zl, gb[nh:], zl])[None, :]
        z8 = jnp.zeros((8 - nh,), F32)
        gbc = jnp.broadcast_to(jnp.concatenate([gb[:nh], z8, gb[nh:], z8])[:, None], (16, tm))
        mqk, mv, mo, ak, gc, qt, vt, gt = _inproj(h, norm1_g[l][None, :], wa, wt, gbr, gbc,
                                                  mw=mw, aw=aw, tm=tm)
        m_out = _mlstm(mqk, mv, mo, gc, gt, conv_w[l], conv_b[l][None, :],
                       mlstm_norm_g[l].reshape(nh, dh), batch=batch, seq=seq, nh=nh, dh=dh, tb=tb)
        lam_init = 0.8 - 0.6 * math.exp(-0.3 * l)
        lamv = jnp.stack([lam_q1[l], lam_k1[l], lam_q2[l], lam_k2[l]]).astype(F32)
        ngb = jnp.broadcast_to(diff_norm_g[l].astype(F32)[:, None], (dv, ATT_BLOCK))
        a_out = _attention(qt, ak, vt, bias_tiles, lamv, ngb, batch=batch, seq=seq, nh=A_HEADS, dv=dv,
                           lam_init=lam_init)
        wr = jnp.concatenate([router_w[l], jnp.zeros((d, LANES - n_exp), F32)], axis=1).astype(BF16)
        br = jnp.concatenate([router_b[l].astype(F32), jnp.full((LANES - n_exp,), NEG, F32)])[None, :]
        h1, u2c, idx, gates, rank, cnt = _router(h, m_out, a_out, w_out[l].astype(BF16), norm2_g[l][None, :],
                                                 wr, br, tm=tm)
        counts = cnt[0, :n_exp].astype(jnp.int32)
        padded = ((counts + r - 1) // r) * r
        pends = jnp.cumsum(padded)
        pstarts = pends - padded
        onehot = idx[:, :TOP_K, None] == jnp.arange(n_exp, dtype=jnp.int32)
        pos = (jnp.sum(jnp.where(onehot, pstarts, 0), axis=-1) + rank[:, :TOP_K]).reshape(-1)
        blk_start = jnp.arange(p_rows // r, dtype=jnp.int32) * r
        blk_e = jnp.minimum(jnp.sum(pends[None, :] <= blk_start[:, None], axis=1), n_exp - 1).astype(jnp.int32)
        n_used = (pends[-1:] // r).astype(jnp.int32)
        nrt = d // LANES
        xs3 = _dispatch(pos, u2c.reshape(t_, nrt, LANES), jnp.zeros((p_rows, nrt, LANES), F32),
                        tmd=_pick_tile(t_, 256))
        ys2 = _experts(blk_e, n_used, xs3.reshape(p_rows * nrt, LANES), w_gu[l].astype(BF16),
                       b_gu[l][:, None, :].astype(F32), w_down[l].astype(BF16), b_down[l][:, None, :].astype(F32))
        h = _combine(pos, h1, gates, final_g[None, :].astype(F32), ys2.reshape(p_rows, nrt, LANES),
                     tmc=_pick_tile(t_, 128), final=(l == depth - 1))
    return h.reshape(batch, seq, d)
```

```python
import functools
import math

import jax
import jax.numpy as jnp
from jax import lax
from jax.experimental import pallas as pl
from jax.experimental.pallas import tpu as pltpu

F32 = jnp.float32
BF16 = jnp.bfloat16
EPS = 1e-5
NEG = -1e30
LOG2E = math.log2(math.e)

CHUNK = 64
M_HEADS = 4
A_HEADS = 4
CONV_K = 4
N_BUCKETS = 32
MAX_DISTANCE = 128
TOP_K = 4
SWIGLU_LIMIT = 7.0
SWIGLU_ALPHA = 1.702

LANES = 128
ROW_TILE = 8
ATT_BLOCK = 256
V_EXTRA = 16
EXPERT_ROWS = 256
VMEM_LIMIT = 56 * 1024 * 1024


def _cparams(sem):
    return pltpu.CompilerParams(dimension_semantics=sem, vmem_limit_bytes=VMEM_LIMIT)


def _dot(a, b):
    return jnp.dot(a, b, preferred_element_type=F32)


def _dot_nt(a, b):
    return lax.dot_general(a, b, (((1,), (1,)), ((), ())), preferred_element_type=F32)


def _log_sigmoid(x):
    return jnp.minimum(x, 0.0) - jnp.log(1.0 + jnp.exp(-jnp.abs(x)))


def _sigmoid(x):
    return 1.0 / (1.0 + jnp.exp(-x))


def _inproj_body(x_ref, g_ref, wa_ref, wt_ref, gbr_ref, gbc_ref,
                 mqk_ref, mv_ref, mo_ref, ak_ref, gc_ref, qt_ref, vt_ref, gt_ref,
                 *, mw, aw, tm):
    x = x_ref[...]
    u = x * lax.rsqrt(jnp.mean(x * x, axis=-1, keepdims=True) + EPS) * g_ref[...]
    ub = u.astype(BF16)

    def mm(a, b):
        return _dot(ub, wa_ref[:, a:b])

    c0 = 2 * mw
    mqk_ref[...] = mm(0, c0).astype(BF16)
    mv_ref[...] = mm(c0, c0 + mw).astype(BF16)
    mo_ref[...] = mm(c0 + mw, c0 + 2 * mw).astype(BF16)
    c1 = c0 + 2 * mw
    ak_ref[...] = mm(c1, c1 + aw).astype(BF16)
    gc_ref[...] = mm(c1 + aw, c1 + aw + 2 * LANES) + gbr_ref[...]
    t = _dot_nt(wt_ref[...], ub)
    qt_ref[...] = t[0:aw].astype(BF16)
    dv = aw // A_HEADS
    ones_rows = (lax.broadcasted_iota(jnp.int32, (V_EXTRA, ATT_BLOCK), 0) == 0).astype(BF16)
    for j in range(tm // ATT_BLOCK):
        for h in range(A_HEADS):
            r0 = h * (dv + V_EXTRA)
            vt_ref[j, r0:r0 + dv, :] = t[aw + h * dv:aw + (h + 1) * dv,
                                         j * ATT_BLOCK:(j + 1) * ATT_BLOCK].astype(BF16)
            vt_ref[j, r0 + dv:r0 + dv + V_EXTRA, :] = ones_rows
    gt_ref[...] = t[2 * aw:2 * aw + 16] + gbc_ref[...]


def _inproj(x2, g1, wa, wt, gbr, gbc, *, mw, aw, tm):
    t_, d = x2.shape
    nt = t_ // tm
    body = functools.partial(_inproj_body, mw=mw, aw=aw, tm=tm)
    row = lambda w: pl.BlockSpec((tm, w), lambda i: (i, 0))
    full = lambda a: pl.BlockSpec(a.shape, lambda i: (0,) * a.ndim)
    return pl.pallas_call(
        body,
        grid=(nt,),
        in_specs=[row(d), full(g1), full(wa), full(wt), full(gbr), full(gbc)],
        out_specs=[row(2 * mw), row(mw), row(mw), row(aw), row(2 * LANES),
                   pl.BlockSpec((aw, tm), lambda i: (0, i)),
                   pl.BlockSpec((tm // ATT_BLOCK, aw + A_HEADS * V_EXTRA, ATT_BLOCK), lambda i: (i, 0, 0)),
                   pl.BlockSpec((16, tm), lambda i: (0, i))],
        out_shape=[jax.ShapeDtypeStruct((t_, 2 * mw), BF16),
                   jax.ShapeDtypeStruct((t_, mw), BF16),
                   jax.ShapeDtypeStruct((t_, mw), BF16),
                   jax.ShapeDtypeStruct((t_, aw), BF16),
                   jax.ShapeDtypeStruct((t_, 2 * LANES), F32),
                   jax.ShapeDtypeStruct((aw, t_), BF16),
                   jax.ShapeDtypeStruct((t_ // ATT_BLOCK, aw + A_HEADS * V_EXTRA, ATT_BLOCK), BF16),
                   jax.ShapeDtypeStruct((16, t_), F32)],
        compiler_params=_cparams(("arbitrary",)),
        name="inproj",
    )(x2, g1, wa, wt, gbr, gbc)


def _segment_scan(x, pos, axis, op, identity):
    d = 1
    while d < CHUNK:
        sh = pltpu.roll(x, d, axis=axis)
        x = op(x, jnp.where(pos >= d, sh, identity))
        d *= 2
    return x


def _mlstm_body(mqk_ref, mv_ref, mo_ref, gc_ref, gt_ref, cw_ref, cb_ref, ng_ref, out_ref,
                ext_ref, qk_ref, kt_ref, gs_ref, cst_ref, mst_ref, *, tb, nh, dh):
    mw = nh * dh

    @pl.when(pl.program_id(1) == 0)
    def _():
        ext_ref[0:8, :] = jnp.zeros((8, 2 * mw), F32)
        cst_ref[...] = jnp.zeros_like(cst_ref)
        mst_ref[...] = jnp.zeros_like(mst_ref)

    ext_ref[8:8 + tb, :] = mqk_ref[...].astype(F32)
    y = cb_ref[...]
    for j in range(CONV_K):
        y = y + cw_ref[j:j + 1, :] * ext_ref[8 - (CONV_K - 1) + j:8 - (CONV_K - 1) + j + tb, :]
    ext_ref[0:8, :] = ext_ref[tb:tb + 8, :]
    act = y * _sigmoid(y)
    qk_ref[:, 0:mw] = act[:, 0:mw].astype(BF16)
    k_all = act[:, mw:2 * mw] * (dh ** -0.5)
    qk_ref[:, mw:2 * mw] = k_all.astype(BF16)
    kt_ref[...] = k_all.T

    gc = gc_ref[...]
    rpos = lax.broadcasted_iota(jnp.int32, (tb, LANES), 0) & (CHUNK - 1)
    bc = _segment_scan(_log_sigmoid(gc[:, LANES:]), rpos, 0, jnp.add, 0.0)
    a_c = gc[:, :LANES] - bc
    pm = _segment_scan(a_c, rpos, 0, jnp.maximum, NEG)
    gs_ref[0] = bc
    gs_ref[1] = a_c
    gs_ref[2] = pm
    gt = gt_ref[...]
    lpos = lax.broadcasted_iota(jnp.int32, (8, tb), 1) & (CHUNK - 1)
    a_r = gt[0:8] - _segment_scan(_log_sigmoid(gt[8:16]), lpos, 1, jnp.add, 0.0)

    tri = (lax.broadcasted_iota(jnp.int32, (CHUNK, CHUNK), 0)
           >= lax.broadcasted_iota(jnp.int32, (CHUNK, CHUNK), 1))
    ones_col = (lax.broadcasted_iota(jnp.int32, (CHUNK, dh), 1) == 0).astype(BF16)
    ones_grp = (lax.broadcasted_iota(jnp.int32, (LANES, dh), 1) == 0).astype(BF16)

    m_state = [mst_ref[h, 0:1, 0:1] for h in range(nh)]
    for c in range(tb // CHUNK):
        r0 = c * CHUNK
        for h in range(nh):
            q = qk_ref[r0:r0 + CHUNK, h * dh:(h + 1) * dh]
            k = qk_ref[r0:r0 + CHUNK, mw + h * dh:mw + (h + 1) * dh]
            vext = jnp.concatenate([mv_ref[r0:r0 + CHUNK, h * dh:(h + 1) * dh], ones_col], axis=1)
            m_prev = m_state[h]
            bc_h = gs_ref[0, r0:r0 + CHUNK, h:h + 1]
            ac_h = gs_ref[1, r0:r0 + CHUNK, h:h + 1]
            pm_h = gs_ref[2, r0:r0 + CHUNK, h:h + 1]
            big_m = jnp.maximum(pm_h, m_prev)
            e = jnp.where(tri, jnp.exp(a_r[h:h + 1, r0:r0 + CHUNK] - big_m), 0.0)
            w = (e * _dot_nt(q, k)).astype(BF16)
            cext = cst_ref[h]
            r = _dot(w, vext) + jnp.exp(m_prev - big_m) * _dot(q, cext.astype(BF16))
            num = r[:, 0:dh]
            den = r[:, dh:dh + 1]
            hh = num / jnp.maximum(jnp.abs(den), jnp.exp(-(bc_h + big_m)))
            yn = hh * lax.rsqrt(jnp.mean(hh * hh, axis=-1, keepdims=True) + EPS) * ng_ref[h:h + 1, :]
            og = _sigmoid(mo_ref[r0:r0 + CHUNK, h * dh:(h + 1) * dh].astype(F32))
            out_ref[r0:r0 + CHUNK, h * dh:(h + 1) * dh] = (og * yn).astype(BF16)
            m_last = jnp.maximum(m_prev, pm_h[CHUNK - 1:CHUNK, :])
            g0 = (r0 // LANES) * LANES
            in_chunk = (lax.broadcasted_iota(jnp.int32, (1, LANES), 1) // CHUNK) == (r0 - g0) // CHUNK
            wrow = jnp.where(in_chunk, jnp.exp(a_r[h:h + 1, g0:g0 + LANES] - m_last), 0.0)
            wkt = (kt_ref[h * dh:(h + 1) * dh, g0:g0 + LANES] * wrow).astype(BF16)
            vgrp = jnp.concatenate([mv_ref[g0:g0 + LANES, h * dh:(h + 1) * dh], ones_grp], axis=1)
            cst_ref[h] = jnp.exp(m_prev - m_last) * cext + _dot(wkt, vgrp)
            m_state[h] = bc_h[CHUNK - 1:CHUNK, :] + m_last
    for h in range(nh):
        mst_ref[h] = jnp.broadcast_to(m_state[h], mst_ref.shape[1:])


def _mlstm(mqk, mv, mo, gc, gt, cw, cb, ng, *, batch, seq, nh, dh, tb):
    mw = nh * dh
    nb = seq // tb
    body = functools.partial(_mlstm_body, tb=tb, nh=nh, dh=dh)
    row = lambda w: pl.BlockSpec((tb, w), lambda b, j: (b * nb + j, 0))
    full = lambda a: pl.BlockSpec(a.shape, lambda b, j: (0,) * a.ndim)
    return pl.pallas_call(
        body,
        grid=(batch, nb),
        in_specs=[row(2 * mw), row(mw), row(mw), row(2 * LANES),
                  pl.BlockSpec((16, tb), lambda b, j: (0, b * nb + j)),
                  full(cw), full(cb), full(ng)],
        out_specs=row(mw),
        out_shape=jax.ShapeDtypeStruct((batch * seq, mw), BF16),
        scratch_shapes=[pltpu.VMEM((tb + 8, 2 * mw), F32),
                        pltpu.VMEM((tb, 2 * mw), BF16),
                        pltpu.VMEM((mw, tb), F32),
                        pltpu.VMEM((3, tb, LANES), F32),
                        pltpu.VMEM((nh, dh, 2 * dh), F32),
                        pltpu.VMEM((nh, 8, LANES), F32)],
        compiler_params=_cparams(("arbitrary", "arbitrary")),
        name="mlstm",
    )(mqk, mv, mo, gc, gt, cw, cb, ng)


def _attn_body(qt_ref, k_ref, vt_ref, bias_ref, lam_ref, ng_ref, out_ref,
               acc_ref, ml_ref, s_ref, p_ref, al_ref, *, lam_init):
    bq = ATT_BLOCK
    qi = pl.program_id(2)
    qt = qt_ref[...]
    half = qt.shape[0] // 2
    rowi = lax.broadcasted_iota(jnp.int32, qt.shape, 0)
    zero = jnp.zeros_like(qt)
    qs = (jnp.where(rowi < half, qt, zero), jnp.where(rowi >= half, qt, zero))
    acc_ref[...] = jnp.zeros_like(acc_ref)
    ml_ref[...] = jnp.full(ml_ref.shape, NEG, F32)
    p_ref[1] = jnp.zeros(p_ref.shape[1:], BF16)
    al_ref[1] = jnp.ones(al_ref.shape[1:], F32)

    def scores(j, slot, biased):
        kb = k_ref[pl.ds(pl.multiple_of(j * bq, bq), bq), :]
        for m in range(2):
            s = _dot(kb, qs[m])
            if biased:
                s = s + bias_ref[0, jnp.clip(j - (qi - 2), 0, 2)]
            s_ref[slot, m] = s

    def softmax(slot):
        for m in range(2):
            s = s_ref[slot, m]
            m_old = ml_ref[m, 0:1, :]
            m_new = jnp.maximum(m_old, jnp.max(s, axis=0, keepdims=True))
            ml_ref[m, 0:1, :] = m_new
            p_ref[slot, m] = jnp.exp2(s - m_new).astype(BF16)
            al_ref[slot, m, 0:1, :] = jnp.exp2(m_old - m_new)

    def values(j, slot):
        vb = vt_ref[j]
        for m in range(2):
            acc_ref[m] = al_ref[slot, m, 0:1, :] * acc_ref[m] + _dot(vb, p_ref[slot, m])

    def step(i, cur, biased):
        scores(jnp.minimum(i + 1, qi), 1 - cur, biased)
        softmax(cur)
        values(jnp.maximum(i - 1, 0), 1 - cur)

    def far_pair(t, carry):
        step(2 * t, 0, False)
        step(2 * t + 1, 1, False)
        return carry

    n_blocks = qi + 1
    n_pairs = lax.shift_right_logical(jnp.maximum(n_blocks - 3, 0), 1)
    tail = n_blocks - 2 * n_pairs
    scores(0, 0, True)
    lax.fori_loop(0, n_pairs, far_pair, 0)
    step(2 * n_pairs, 0, True)
    for idx in range(1, 4):
        @pl.when(idx < tail)
        def _(idx=idx):
            step(2 * n_pairs + idx, idx % 2, True)

    @pl.when((tail & 1) == 1)
    def _():
        values(qi, 0)

    @pl.when((tail & 1) == 0)
    def _():
        values(qi, 1)

    lv = lam_ref[...]
    lam = (jnp.exp(jnp.sum(lv[0:1] * lv[1:2], axis=-1, keepdims=True))
           - jnp.exp(jnp.sum(lv[2:3] * lv[3:4], axis=-1, keepdims=True)) + lam_init)
    dv = out_ref.shape[1]
    o = (acc_ref[0, 0:dv, :] / acc_ref[0, dv:dv + 1, :]
         - lam * (acc_ref[1, 0:dv, :] / acc_ref[1, dv:dv + 1, :]))
    yn = o * lax.rsqrt(jnp.mean(o * o, axis=0, keepdims=True) + EPS) * ng_ref[...] * (1.0 - lam_init)
    out_ref[...] = yn.T.astype(BF16)


def _attention(qt, ak, vt, bias, lamv, ngb, *, batch, seq, nh, dv, lam_init):
    bq = ATT_BLOCK
    nq = seq // bq
    body = functools.partial(_attn_body, lam_init=lam_init)
    return pl.pallas_call(
        body,
        grid=(batch, nh, nq),
        in_specs=[pl.BlockSpec((dv, bq), lambda b, h, i: (h, b * nq + i)),
                  pl.BlockSpec((seq, dv), lambda b, h, i: (b, h)),
                  pl.BlockSpec((nq, dv + V_EXTRA, bq), lambda b, h, i: (b, h, 0)),
                  pl.BlockSpec((1, 3, bq, bq), lambda b, h, i: (h, 0, 0, 0)),
                  pl.BlockSpec(lamv.shape, lambda b, h, i: (0, 0)),
                  pl.BlockSpec(ngb.shape, lambda b, h, i: (0, 0))],
        out_specs=pl.BlockSpec((bq, dv), lambda b, h, i: (b * nq + i, h)),
        out_shape=jax.ShapeDtypeStruct((batch * seq, nh * dv), BF16),
        scratch_shapes=[pltpu.VMEM((2, dv + V_EXTRA, bq), F32), pltpu.VMEM((2, 8, bq), F32),
                        pltpu.VMEM((2, 2, bq, bq), F32), pltpu.VMEM((2, 2, bq, bq), BF16),
                        pltpu.VMEM((2, 2, 8, bq), F32)],
        compiler_params=_cparams(("arbitrary", "arbitrary", "arbitrary")),
        name="diffattn",
    )(qt, ak, vt, bias, lamv, ngb)


def _rel_bucket(rel):
    nb = N_BUCKETS // 2
    max_exact = nb // 2
    ret = (rel > 0).astype(jnp.int32) * nb
    n = jnp.abs(rel)
    nf = jnp.maximum(n, 1).astype(F32)
    large = max_exact + (jnp.log(nf / max_exact) / math.log(MAX_DISTANCE / max_exact)
                         * (nb - max_exact)).astype(jnp.int32)
    large = jnp.minimum(large, nb - 1)
    return ret + jnp.where(n < max_exact, n, large)


def _bias_tiles(rel_bias):
    b = ATT_BLOCK
    table = rel_bias.astype(F32)

    def lookup(rel):
        onehot = _rel_bucket(rel)[..., None] == jnp.arange(N_BUCKETS, dtype=jnp.int32)
        return jnp.sum(jnp.where(onehot[..., None], table, 0.0), axis=-2)

    kpos = jnp.arange(b, dtype=jnp.int32)[:, None]
    qpos = jnp.arange(b, dtype=jnp.int32)[None, :]
    far = lookup(jnp.array(-(2 * b), jnp.int32))
    prev = (lookup(kpos - b - qpos) - far) * LOG2E
    diag = (lookup(kpos - qpos) - far) * LOG2E
    diag = jnp.where((kpos // CHUNK <= qpos // CHUNK)[..., None], diag, NEG)
    return jnp.transpose(jnp.stack([jnp.zeros_like(prev), prev, diag], axis=0), (3, 0, 1, 2))


def _router_body(x_ref, mo_ref, ao_ref, wo_ref, g2_ref, wr_ref, br_ref, tril_ref,
                 h1_ref, u2_ref, idx_ref, gate_ref, rank_ref, cnt_ref, run_ref, *, mw):
    @pl.when(pl.program_id(0) == 0)
    def _():
        run_ref[...] = jnp.zeros_like(run_ref)

    h1 = x_ref[...] + _dot(mo_ref[...], wo_ref[0:mw, :]) + _dot(ao_ref[...], wo_ref[mw:, :])
    h1_ref[...] = h1
    u2 = h1 * lax.rsqrt(jnp.mean(h1 * h1, axis=-1, keepdims=True) + EPS) * g2_ref[...]
    for j in range(u2.shape[1] // LANES):
        u2_ref[pl.ds(j, u2.shape[0], stride=ROW_TILE), :] = u2[:, j * LANES:(j + 1) * LANES]
    logits = _dot(u2.astype(BF16), wr_ref[...]) + br_ref[...]
    lane = lax.broadcasted_iota(jnp.int32, logits.shape, 1).astype(F32)
    l = logits
    vals, sels, idxs = [], [], []
    for _ in range(TOP_K):
        mx = jnp.max(l, axis=-1, keepdims=True)
        ik = jnp.min(jnp.where(l == mx, lane, float(LANES)), axis=-1, keepdims=True)
        sel = lane == ik
        l = jnp.where(sel, -jnp.inf, l)
        vals.append(mx)
        idxs.append(ik)
        sels.append(sel)
    es = [jnp.exp(v - vals[0]) for v in vals]
    inv = 1.0 / sum(es)
    oh = sum(s.astype(F32) for s in sels)
    tot = _dot(tril_ref[...], oh.astype(BF16)) + run_ref[...]
    ranks = [jnp.sum(jnp.where(s, tot, 0.0), axis=-1, keepdims=True) for s in sels]
    run_ref[...] = run_ref[...] + jnp.sum(oh, axis=0, keepdims=True)
    cnt_ref[...] = run_ref[...]

    def pack(cols):
        out = jnp.zeros(logits.shape, F32)
        for j, cj in enumerate(cols):
            out = jnp.where(lane == float(j), cj, out)
        return out

    idx_ref[...] = pack(idxs).astype(jnp.int32)
    gate_ref[...] = pack([e * inv for e in es])
    rank_ref[...] = pack(ranks).astype(jnp.int32)


def _router(x2, m_out, a_out, wo, g2, wr, br, *, tm):
    t_, d = x2.shape
    mw = m_out.shape[1]
    nt = t_ // tm
    tril = (jnp.arange(tm)[:, None] > jnp.arange(tm)[None, :]).astype(BF16)
    body = functools.partial(_router_body, mw=mw)
    row = lambda w: pl.BlockSpec((tm, w), lambda i: (i, 0))
    full = lambda a: pl.BlockSpec(a.shape, lambda i: (0,) * a.ndim)
    return pl.pallas_call(
        body,
        grid=(nt,),
        in_specs=[row(d), row(mw), row(a_out.shape[1]), full(wo), full(g2), full(wr), full(br), full(tril)],
        out_specs=[row(d), pl.BlockSpec((tm * d // LANES, LANES), lambda i: (i, 0)),
                   row(LANES), row(LANES), row(LANES),
                   pl.BlockSpec((1, LANES), lambda i: (0, 0))],
        out_shape=[jax.ShapeDtypeStruct((t_, d), F32),
                   jax.ShapeDtypeStruct((t_ * d // LANES, LANES), F32),
                   jax.ShapeDtypeStruct((t_, LANES), jnp.int32),
                   jax.ShapeDtypeStruct((t_, LANES), F32),
                   jax.ShapeDtypeStruct((t_, LANES), jnp.int32),
                   jax.ShapeDtypeStruct((1, LANES), F32)],
        scratch_shapes=[pltpu.VMEM((1, LANES), F32)],
        compiler_params=_cparams(("arbitrary",)),
        name="router",
    )(x2, m_out, a_out, wo, g2, wr, br, tril)


def _pair_loop(n_tok, fn):
    def body(r, c):
        for k in range(TOP_K):
            fn(r, k)
        return c
    lax.fori_loop(0, n_tok, body, 0, unroll=4)


def _dispatch_body(plo_ref, phi_ref, nu_ref, pos_ref, u2_ref, xs_ref, zero_ref, sem, zsem, *, tmd):
    @pl.when(pl.program_id(0) == 0)
    def _():
        zero_ref[...] = jnp.zeros_like(zero_ref)
        r = zero_ref.shape[0]

        def zrow(row):
            return pltpu.make_async_copy(zero_ref.at[0], xs_ref.at[row], zsem)

        def zblock(blk):
            return pltpu.make_async_copy(zero_ref, xs_ref.at[pl.ds(pl.multiple_of(blk * r, r), r)], zsem)

        def fill(row_fn, blk_fn):
            def per_expert(e, c):
                return lax.fori_loop(plo_ref[e], phi_ref[e], lambda row, c2: (row_fn(row), c2)[1], c)
            lax.fori_loop(0, plo_ref.shape[0], per_expert, 0)
            lax.fori_loop(nu_ref[0], xs_ref.shape[0] // r, lambda blk, c: (blk_fn(blk), c)[1], 0)

        fill(lambda row: zrow(row).start(), lambda blk: zblock(blk).start())
        fill(lambda row: zrow(row).wait(), lambda blk: zblock(blk).wait())

    def copy(r, k):
        return pltpu.make_async_copy(u2_ref.at[r], xs_ref.at[pos_ref[0, 0, r * TOP_K + k]], sem)

    _pair_loop(tmd, lambda r, k: copy(r, k).start(priority=k % 2))
    _pair_loop(tmd, lambda r, k: copy(r, k).wait())


def _dispatch(pad_lo, pad_hi, n_used, pos, u2c, p_rows, *, tmd):
    t_ = u2c.shape[0]
    nt = t_ // tmd
    pos3 = pos.reshape(nt, 1, tmd * TOP_K)
    grid_spec = pltpu.PrefetchScalarGridSpec(
        num_scalar_prefetch=3,
        grid=(nt,),
        in_specs=[pl.BlockSpec((1, 1, tmd * TOP_K), lambda i, lo, hi, nu: (i, 0, 0), memory_space=pltpu.SMEM),
                  pl.BlockSpec((tmd,) + u2c.shape[1:], lambda i, lo, hi, nu: (i, 0, 0))],
        out_specs=pl.BlockSpec(memory_space=pl.ANY),
        scratch_shapes=[pltpu.VMEM((EXPERT_ROWS,) + u2c.shape[1:], F32),
                        pltpu.SemaphoreType.DMA, pltpu.SemaphoreType.DMA],
    )
    return pl.pallas_call(
        functools.partial(_dispatch_body, tmd=tmd),
        grid_spec=grid_spec,
        out_shape=jax.ShapeDtypeStruct((p_rows,) + u2c.shape[1:], F32),
        compiler_params=_cparams(("arbitrary",)),
        name="moe_dispatch",
    )(pad_lo, pad_hi, n_used, pos3, u2c)


def _experts_body(be_ref, nu_ref, xs_ref, wgu_ref, bgu_ref, wd_ref, bd_ref, ys_ref, wgu_bf, wd_bf, *, dff, r):
    i = pl.program_id(0)
    nj = wd_ref.shape[2] // LANES
    used = i < nu_ref[0]

    @pl.when(used & ((i == 0) | (be_ref[i] != be_ref[jnp.maximum(i - 1, 0)])))
    def _():
        wgu_bf[...] = wgu_ref[0].astype(BF16)
        wd_bf[...] = wd_ref[0].astype(BF16)

    @pl.when(used)
    def _():
        x = jnp.concatenate([xs_ref[pl.ds(j, r, stride=ROW_TILE), :] for j in range(nj)], axis=1)
        gu = _dot(x.astype(BF16), wgu_bf[...]) + bgu_ref[0]
        gate = jnp.minimum(gu[:, :dff], SWIGLU_LIMIT)
        up = jnp.clip(gu[:, dff:], -SWIGLU_LIMIT, SWIGLU_LIMIT)
        act = (up + 1.0) * gate * _sigmoid(SWIGLU_ALPHA * gate)
        y = _dot(act.astype(BF16), wd_bf[...]) + bd_ref[0]
        for j in range(nj):
            ys_ref[pl.ds(j, r, stride=ROW_TILE), :] = y[:, j * LANES:(j + 1) * LANES]

    @pl.when(jnp.logical_not(used))
    def _():
        ys_ref[...] = jnp.zeros_like(ys_ref)


def _experts(blk_e, n_used, xs2, wgu, bgu, wd, bd):
    r = EXPERT_ROWS
    dff, d = wd.shape[1], wd.shape[2]
    rt = r * d // LANES
    grid_spec = pltpu.PrefetchScalarGridSpec(
        num_scalar_prefetch=2,
        grid=(xs2.shape[0] // rt,),
        in_specs=[pl.BlockSpec((rt, LANES), lambda i, be, nu: (i, 0)),
                  pl.BlockSpec((1, d, 2 * dff), lambda i, be, nu: (be[i], 0, 0)),
                  pl.BlockSpec((1, 1, 2 * dff), lambda i, be, nu: (be[i], 0, 0)),
                  pl.BlockSpec((1, dff, d), lambda i, be, nu: (be[i], 0, 0)),
                  pl.BlockSpec((1, 1, d), lambda i, be, nu: (be[i], 0, 0))],
        out_specs=pl.BlockSpec((rt, LANES), lambda i, be, nu: (i, 0)),
        scratch_shapes=[pltpu.VMEM((d, 2 * dff), BF16), pltpu.VMEM((dff, d), BF16)],
    )
    return pl.pallas_call(
        functools.partial(_experts_body, dff=dff, r=r),
        grid_spec=grid_spec,
        out_shape=jax.ShapeDtypeStruct(xs2.shape, F32),
        compiler_params=_cparams(("arbitrary",)),
        name="moe_experts",
    )(blk_e, n_used, xs2, wgu, bgu, wd, bd)


def _combine_body(pos_ref, posn_ref, h1_ref, gate_ref, fg_ref, ys_ref, out_ref, rows_ref, sems,
                  *, tmc, final):
    i = pl.program_id(0)
    slot = i % 2

    def copy(pref, sl, r, k):
        return pltpu.make_async_copy(
            ys_ref.at[pref[0, 0, r * TOP_K + k]],
            rows_ref.at[sl, k, pl.ds(pl.multiple_of(r * ROW_TILE, ROW_TILE), ROW_TILE)],
            sems.at[sl])

    @pl.when(i == 0)
    def _():
        _pair_loop(tmc, lambda r, k: copy(pos_ref, 0, r, k).start(priority=k % 2))

    @pl.when(i + 1 < pl.num_programs(0))
    def _():
        _pair_loop(tmc, lambda r, k: copy(posn_ref, 1 - slot, r, k).start(priority=k % 2))

    _pair_loop(tmc, lambda r, k: copy(pos_ref, slot, r, k).wait())
    g = gate_ref[...]
    cols = []
    for j in range(h1_ref.shape[1] // LANES):
        hj = h1_ref[:, j * LANES:(j + 1) * LANES]
        for k in range(TOP_K):
            hj = hj + g[:, k:k + 1] * rows_ref[slot, k, pl.ds(j, tmc, stride=ROW_TILE), :]
        cols.append(hj)
    h = jnp.concatenate(cols, axis=1)
    if final:
        h = h * lax.rsqrt(jnp.mean(h * h, axis=-1, keepdims=True) + EPS) * fg_ref[...]
    out_ref[...] = h


def _combine(pos, h1, gates, fg, ys3, *, tmc, final):
    t_, d = h1.shape
    nt = t_ // tmc
    pos3 = pos.reshape(nt, 1, tmc * TOP_K)
    row = lambda w: pl.BlockSpec((tmc, w), lambda i: (i, 0))
    smem = lambda f: pl.BlockSpec((1, 1, tmc * TOP_K), f, memory_space=pltpu.SMEM)
    return pl.pallas_call(
        functools.partial(_combine_body, tmc=tmc, final=final),
        grid=(nt,),
        in_specs=[smem(lambda i: (i, 0, 0)), smem(lambda i: (jnp.minimum(i + 1, nt - 1), 0, 0)),
                  row(d), row(LANES), pl.BlockSpec(fg.shape, lambda i: (0, 0)),
                  pl.BlockSpec(memory_space=pl.ANY)],
        out_specs=row(d),
        out_shape=jax.ShapeDtypeStruct((t_, d), F32),
        scratch_shapes=[pltpu.VMEM((2, TOP_K, tmc * ROW_TILE, LANES), F32), pltpu.SemaphoreType.DMA((2,))],
        compiler_params=_cparams(("arbitrary",)),
        name="moe_combine",
    )(pos3, pos3, h1, gates, fg, ys3)


def _pick_tile(n, pref):
    t = pref
    while n % t:
        t //= 2
    return t


def kernel(x, rel_bias, norm1_g, w_in, conv_w, conv_b, gate_b, mlstm_norm_g, lam_q1, lam_k1, lam_q2,
           lam_k2, diff_norm_g, w_out, norm2_g, router_w, router_b, w_gu, b_gu, w_down, b_down, final_g):
    batch, seq, d = x.shape
    depth = w_in.shape[0]
    t_ = batch * seq
    nh = M_HEADS
    mw = conv_w.shape[-1] // 2
    dh = mw // nh
    aw = w_out.shape[1] - mw
    dv = aw // A_HEADS
    n_exp = router_w.shape[-1]
    dff = w_down.shape[-2]
    assert seq % ATT_BLOCK == 0 and dh == LANES and dv == LANES and n_exp <= LANES
    assert d == ROW_TILE * LANES, 'token rows are moved as single (8, 128) f32 tiles'
    tm = _pick_tile(t_, 512)
    tb = ATT_BLOCK
    r = EXPERT_ROWS
    p_rows = t_ * TOP_K + n_exp * r

    bias_tiles = _bias_tiles(rel_bias)
    h = x.reshape(t_, d)
    for l in range(depth):
        w = w_in[l]
        c0 = 4 * mw
        c1 = c0 + 2 * nh
        zc = jnp.zeros((d, LANES - nh), F32)
        wa = jnp.concatenate([w[:, 0:2 * mw], w[:, 2 * mw:3 * mw], w[:, 3 * mw:c0],
                              w[:, c1 + aw:c1 + 2 * aw],
                              w[:, c0:c0 + nh], zc, w[:, c0 + nh:c1], zc], axis=1).astype(BF16)
        z4 = jnp.zeros((d, 8 - nh), F32)
        wt = jnp.concatenate([w[:, c1:c1 + aw] * ((dv // 2) ** -0.5 * LOG2E), w[:, c1 + 2 * aw:c1 + 3 * aw],
                              w[:, c0:c0 + nh], z4, w[:, c0 + nh:c1], z4], axis=1).T.astype(BF16)
        gb = gate_b[l].astype(F32)
        zl = jnp.zeros((LANES - nh,), F32)
        gbr = jnp.concatenate([gb[:nh], zl, gb[nh:], zl])[None, :]
        z8 = jnp.zeros((8 - nh,), F32)
        gbc = jnp.broadcast_to(jnp.concatenate([gb[:nh], z8, gb[nh:], z8])[:, None], (16, tm))
        mqk, mv, mo, ak, gc, qt, vt, gt = _inproj(h, norm1_g[l][None, :], wa, wt, gbr, gbc,
                                                  mw=mw, aw=aw, tm=tm)
        m_out = _mlstm(mqk, mv, mo, gc, gt, conv_w[l], conv_b[l][None, :],
                       mlstm_norm_g[l].reshape(nh, dh), batch=batch, seq=seq, nh=nh, dh=dh, tb=tb)
        lam_init = 0.8 - 0.6 * math.exp(-0.3 * l)
        lamv = jnp.stack([lam_q1[l], lam_k1[l], lam_q2[l], lam_k2[l]]).astype(F32)
        ngb = jnp.broadcast_to(diff_norm_g[l].astype(F32)[:, None], (dv, ATT_BLOCK))
        a_out = _attention(qt, ak, vt, bias_tiles, lamv, ngb, batch=batch, seq=seq, nh=A_HEADS, dv=dv,
                           lam_init=lam_init)
        wr = jnp.concatenate([router_w[l], jnp.zeros((d, LANES - n_exp), F32)], axis=1).astype(BF16)
        br = jnp.concatenate([router_b[l].astype(F32), jnp.full((LANES - n_exp,), NEG, F32)])[None, :]
        h1, u2c, idx, gates, rank, cnt = _router(h, m_out, a_out, w_out[l].astype(BF16), norm2_g[l][None, :],
                                                 wr, br, tm=tm)
        counts = cnt[0, :n_exp].astype(jnp.int32)
        padded = ((counts + r - 1) // r) * r
        pends = jnp.cumsum(padded)
        pstarts = pends - padded
        onehot = idx[:, :TOP_K, None] == jnp.arange(n_exp, dtype=jnp.int32)
        pos = (jnp.sum(jnp.where(onehot, pstarts, 0), axis=-1) + rank[:, :TOP_K]).reshape(-1)
        blk_start = jnp.arange(p_rows // r, dtype=jnp.int32) * r
        blk_e = jnp.minimum(jnp.sum(pends[None, :] <= blk_start[:, None], axis=1), n_exp - 1).astype(jnp.int32)
        n_used = (pends[-1:] // r).astype(jnp.int32)
        nrt = d // LANES
        xs3 = _dispatch(pstarts + counts, pends, n_used, pos, u2c.reshape(t_, nrt, LANES), p_rows,
                        tmd=_pick_tile(t_, 256))
        ys2 = _experts(blk_e, n_used, xs3.reshape(p_rows * nrt, LANES), w_gu[l].astype(F32),
                       b_gu[l][:, None, :].astype(F32), w_down[l].astype(F32), b_down[l][:, None, :].astype(F32))
        h = _combine(pos, h1, gates, final_g[None, :].astype(F32), ys2.reshape(p_rows, nrt, LANES),
                     tmc=_pick_tile(t_, 128), final=(l == depth - 1))
    return h.reshape(batch, seq, d)
```

```python
import functools
import math

import jax
import jax.numpy as jnp
from jax import lax
from jax.experimental import pallas as pl
from jax.experimental.pallas import tpu as pltpu

F32 = jnp.float32
BF16 = jnp.bfloat16
EPS = 1e-5
NEG = -1e30
LOG2E = math.log2(math.e)

CHUNK = 64
M_HEADS = 4
A_HEADS = 4
CONV_K = 4
N_BUCKETS = 32
MAX_DISTANCE = 128
TOP_K = 4
SWIGLU_LIMIT = 7.0
SWIGLU_ALPHA = 1.702

LANES = 128
ROW_TILE = 8
ATT_BLOCK = 512
MLSTM_BLOCK = 256
V_EXTRA = 16
EXPERT_ROWS = 256
VMEM_LIMIT = 56 * 1024 * 1024


def _cparams(sem):
    return pltpu.CompilerParams(dimension_semantics=sem, vmem_limit_bytes=VMEM_LIMIT)


def _dot(a, b):
    return jnp.dot(a, b, preferred_element_type=F32)


def _dot_nt(a, b):
    return lax.dot_general(a, b, (((1,), (1,)), ((), ())), preferred_element_type=F32)


def _log_sigmoid(x):
    return jnp.minimum(x, 0.0) - jnp.log(1.0 + jnp.exp(-jnp.abs(x)))


def _sigmoid(x):
    return 1.0 / (1.0 + jnp.exp(-x))


def _inproj_body(x_ref, g_ref, wa_ref, wt_ref, gbr_ref, gbc_ref,
                 mqk_ref, mv_ref, mo_ref, ak_ref, gc_ref, qt_ref, vt_ref, gt_ref,
                 *, mw, aw, tm):
    x = x_ref[...]
    u = x * lax.rsqrt(jnp.mean(x * x, axis=-1, keepdims=True) + EPS) * g_ref[...]
    ub = u.astype(BF16)

    def mm(a, b):
        return _dot(ub, wa_ref[:, a:b])

    c0 = 2 * mw
    mqk_ref[...] = mm(0, c0).astype(BF16)
    mv_ref[...] = mm(c0, c0 + mw).astype(BF16)
    mo_ref[...] = mm(c0 + mw, c0 + 2 * mw).astype(BF16)
    c1 = c0 + 2 * mw
    ak_ref[...] = mm(c1, c1 + aw).astype(BF16)
    gc_ref[...] = mm(c1 + aw, c1 + aw + 2 * LANES) + gbr_ref[...]
    t = _dot_nt(wt_ref[...], ub)
    qt_ref[...] = t[0:aw].astype(BF16)
    dv = aw // A_HEADS
    ones_rows = (lax.broadcasted_iota(jnp.int32, (V_EXTRA, ATT_BLOCK), 0) == 0).astype(BF16)
    for j in range(tm // ATT_BLOCK):
        for h in range(A_HEADS):
            r0 = h * (dv + V_EXTRA)
            vt_ref[j, r0:r0 + dv, :] = t[aw + h * dv:aw + (h + 1) * dv,
                                         j * ATT_BLOCK:(j + 1) * ATT_BLOCK].astype(BF16)
            vt_ref[j, r0 + dv:r0 + dv + V_EXTRA, :] = ones_rows
    gt_ref[...] = t[2 * aw:2 * aw + 16] + gbc_ref[...]


def _inproj(x2, g1, wa, wt, gbr, gbc, *, mw, aw, tm):
    t_, d = x2.shape
    nt = t_ // tm
    body = functools.partial(_inproj_body, mw=mw, aw=aw, tm=tm)
    row = lambda w: pl.BlockSpec((tm, w), lambda i: (i, 0))
    full = lambda a: pl.BlockSpec(a.shape, lambda i: (0,) * a.ndim)
    return pl.pallas_call(
        body,
        grid=(nt,),
        in_specs=[row(d), full(g1), full(wa), full(wt), full(gbr), full(gbc)],
        out_specs=[row(2 * mw), row(mw), row(mw), row(aw), row(2 * LANES),
                   pl.BlockSpec((aw, tm), lambda i: (0, i)),
                   pl.BlockSpec((tm // ATT_BLOCK, aw + A_HEADS * V_EXTRA, ATT_BLOCK), lambda i: (i, 0, 0)),
                   pl.BlockSpec((16, tm), lambda i: (0, i))],
        out_shape=[jax.ShapeDtypeStruct((t_, 2 * mw), BF16),
                   jax.ShapeDtypeStruct((t_, mw), BF16),
                   jax.ShapeDtypeStruct((t_, mw), BF16),
                   jax.ShapeDtypeStruct((t_, aw), BF16),
                   jax.ShapeDtypeStruct((t_, 2 * LANES), F32),
                   jax.ShapeDtypeStruct((aw, t_), BF16),
                   jax.ShapeDtypeStruct((t_ // ATT_BLOCK, aw + A_HEADS * V_EXTRA, ATT_BLOCK), BF16),
                   jax.ShapeDtypeStruct((16, t_), F32)],
        compiler_params=_cparams(("arbitrary",)),
        name="inproj",
    )(x2, g1, wa, wt, gbr, gbc)


def _segment_scan(x, pos, axis, op, identity):
    d = 1
    while d < CHUNK:
        sh = pltpu.roll(x, d, axis=axis)
        x = op(x, jnp.where(pos >= d, sh, identity))
        d *= 2
    return x


def _mlstm_body(mqk_ref, mv_ref, mo_ref, gc_ref, gt_ref, cw_ref, cb_ref, ng_ref, out_ref,
                ext_ref, qk_ref, kt_ref, gs_ref, cst_ref, mst_ref, *, tb, nh, dh):
    mw = nh * dh

    @pl.when(pl.program_id(1) == 0)
    def _():
        ext_ref[0:8, :] = jnp.zeros((8, 2 * mw), F32)
        cst_ref[...] = jnp.zeros_like(cst_ref)
        mst_ref[...] = jnp.zeros_like(mst_ref)

    ext_ref[8:8 + tb, :] = mqk_ref[...].astype(F32)
    y = cb_ref[...]
    for j in range(CONV_K):
        y = y + cw_ref[j:j + 1, :] * ext_ref[8 - (CONV_K - 1) + j:8 - (CONV_K - 1) + j + tb, :]
    ext_ref[0:8, :] = ext_ref[tb:tb + 8, :]
    act = y * _sigmoid(y)
    qk_ref[:, 0:mw] = act[:, 0:mw].astype(BF16)
    k_all = act[:, mw:2 * mw] * (dh ** -0.5)
    qk_ref[:, mw:2 * mw] = k_all.astype(BF16)
    kt_ref[...] = k_all.T

    gc = gc_ref[...]
    rpos = lax.broadcasted_iota(jnp.int32, (tb, LANES), 0) & (CHUNK - 1)
    bc = _segment_scan(_log_sigmoid(gc[:, LANES:]), rpos, 0, jnp.add, 0.0)
    a_c = gc[:, :LANES] - bc
    pm = _segment_scan(a_c, rpos, 0, jnp.maximum, NEG)
    gs_ref[0] = bc
    gs_ref[1] = a_c
    gs_ref[2] = pm
    gt = gt_ref[...]
    lpos = lax.broadcasted_iota(jnp.int32, (8, tb), 1) & (CHUNK - 1)
    a_r = gt[0:8] - _segment_scan(_log_sigmoid(gt[8:16]), lpos, 1, jnp.add, 0.0)

    tri = (lax.broadcasted_iota(jnp.int32, (CHUNK, CHUNK), 0)
           >= lax.broadcasted_iota(jnp.int32, (CHUNK, CHUNK), 1))
    ones_col = (lax.broadcasted_iota(jnp.int32, (CHUNK, dh), 1) == 0).astype(BF16)
    ones_grp = (lax.broadcasted_iota(jnp.int32, (LANES, dh), 1) == 0).astype(BF16)

    m_state = [mst_ref[h, 0:1, 0:1] for h in range(nh)]
    for c in range(tb // CHUNK):
        r0 = c * CHUNK
        for h in range(nh):
            q = qk_ref[r0:r0 + CHUNK, h * dh:(h + 1) * dh]
            k = qk_ref[r0:r0 + CHUNK, mw + h * dh:mw + (h + 1) * dh]
            vext = jnp.concatenate([mv_ref[r0:r0 + CHUNK, h * dh:(h + 1) * dh], ones_col], axis=1)
            m_prev = m_state[h]
            bc_h = gs_ref[0, r0:r0 + CHUNK, h:h + 1]
            ac_h = gs_ref[1, r0:r0 + CHUNK, h:h + 1]
            pm_h = gs_ref[2, r0:r0 + CHUNK, h:h + 1]
            big_m = jnp.maximum(pm_h, m_prev)
            e = jnp.where(tri, jnp.exp(a_r[h:h + 1, r0:r0 + CHUNK] - big_m), 0.0)
            w = (e * _dot_nt(q, k)).astype(BF16)
            cext = cst_ref[h]
            r = _dot(w, vext) + jnp.exp(m_prev - big_m) * _dot(q, cext.astype(BF16))
            num = r[:, 0:dh]
            den = r[:, dh:dh + 1]
            hh = num / jnp.maximum(jnp.abs(den), jnp.exp(-(bc_h + big_m)))
            yn = hh * lax.rsqrt(jnp.mean(hh * hh, axis=-1, keepdims=True) + EPS) * ng_ref[h:h + 1, :]
            og = _sigmoid(mo_ref[r0:r0 + CHUNK, h * dh:(h + 1) * dh].astype(F32))
            out_ref[r0:r0 + CHUNK, h * dh:(h + 1) * dh] = (og * yn).astype(BF16)
            m_last = jnp.maximum(m_prev, pm_h[CHUNK - 1:CHUNK, :])
            g0 = (r0 // LANES) * LANES
            in_chunk = (lax.broadcasted_iota(jnp.int32, (1, LANES), 1) // CHUNK) == (r0 - g0) // CHUNK
            wrow = jnp.where(in_chunk, jnp.exp(a_r[h:h + 1, g0:g0 + LANES] - m_last), 0.0)
            wkt = (kt_ref[h * dh:(h + 1) * dh, g0:g0 + LANES] * wrow).astype(BF16)
            vgrp = jnp.concatenate([mv_ref[g0:g0 + LANES, h * dh:(h + 1) * dh], ones_grp], axis=1)
            cst_ref[h] = jnp.exp(m_prev - m_last) * cext + _dot(wkt, vgrp)
            m_state[h] = bc_h[CHUNK - 1:CHUNK, :] + m_last
    for h in range(nh):
        mst_ref[h] = jnp.broadcast_to(m_state[h], mst_ref.shape[1:])


def _mlstm(mqk, mv, mo, gc, gt, cw, cb, ng, *, batch, seq, nh, dh, tb):
    mw = nh * dh
    nb = seq // tb
    body = functools.partial(_mlstm_body, tb=tb, nh=nh, dh=dh)
    row = lambda w: pl.BlockSpec((tb, w), lambda b, j: (b * nb + j, 0))
    full = lambda a: pl.BlockSpec(a.shape, lambda b, j: (0,) * a.ndim)
    return pl.pallas_call(
        body,
        grid=(batch, nb),
        in_specs=[row(2 * mw), row(mw), row(mw), row(2 * LANES),
                  pl.BlockSpec((16, tb), lambda b, j: (0, b * nb + j)),
                  full(cw), full(cb), full(ng)],
        out_specs=row(mw),
        out_shape=jax.ShapeDtypeStruct((batch * seq, mw), BF16),
        scratch_shapes=[pltpu.VMEM((tb + 8, 2 * mw), F32),
                        pltpu.VMEM((tb, 2 * mw), BF16),
                        pltpu.VMEM((mw, tb), F32),
                        pltpu.VMEM((3, tb, LANES), F32),
                        pltpu.VMEM((nh, dh, 2 * dh), F32),
                        pltpu.VMEM((nh, 8, LANES), F32)],
        compiler_params=_cparams(("arbitrary", "arbitrary")),
        name="mlstm",
    )(mqk, mv, mo, gc, gt, cw, cb, ng)


def _attn_body(qt_ref, k_ref, vt_ref, bias_ref, lam_ref, ng_ref, out_ref,
               acc_ref, ml_ref, s_ref, p_ref, al_ref, *, lam_init):
    bq = ATT_BLOCK
    qi = pl.program_id(2)
    qt = qt_ref[...]
    half = qt.shape[0] // 2
    rowi = lax.broadcasted_iota(jnp.int32, qt.shape, 0)
    zero = jnp.zeros_like(qt)
    qs = (jnp.where(rowi < half, qt, zero), jnp.where(rowi >= half, qt, zero))
    acc_ref[...] = jnp.zeros_like(acc_ref)
    ml_ref[...] = jnp.full(ml_ref.shape, NEG, F32)
    p_ref[1] = jnp.zeros(p_ref.shape[1:], BF16)
    al_ref[1] = jnp.ones(al_ref.shape[1:], F32)

    def scores(j, slot, biased):
        kb = k_ref[pl.ds(pl.multiple_of(j * bq, bq), bq), :]
        for m in range(2):
            s = _dot(kb, qs[m])
            if biased:
                s = s + bias_ref[0, jnp.clip(j - (qi - 2), 0, 2)]
            s_ref[slot, m] = s

    def softmax(slot):
        for m in range(2):
            s = s_ref[slot, m]
            m_old = ml_ref[m, 0:1, :]
            m_new = jnp.maximum(m_old, jnp.max(s, axis=0, keepdims=True))
            ml_ref[m, 0:1, :] = m_new
            p_ref[slot, m] = jnp.exp2(s - m_new).astype(BF16)
            al_ref[slot, m, 0:1, :] = jnp.exp2(m_old - m_new)

    def values(j, slot):
        vb = vt_ref[j]
        for m in range(2):
            acc_ref[m] = al_ref[slot, m, 0:1, :] * acc_ref[m] + _dot(vb, p_ref[slot, m])

    def step(i, cur, biased):
        scores(jnp.minimum(i + 1, qi), 1 - cur, biased)
        softmax(cur)
        values(jnp.maximum(i - 1, 0), 1 - cur)

    def far_pair(t, carry):
        step(2 * t, 0, False)
        step(2 * t + 1, 1, False)
        return carry

    def near_pair(t, carry):
        step(2 * t, 0, True)
        step(2 * t + 1, 1, True)
        return carry

    n_blocks = qi + 1
    n_pairs = lax.shift_right_logical(n_blocks, 1)
    n_far = jnp.minimum(lax.shift_right_logical(jnp.maximum(n_blocks - 3, 0), 1), n_pairs)
    scores(0, 0, True)
    lax.fori_loop(0, n_far, far_pair, 0)
    lax.fori_loop(n_far, n_pairs, near_pair, 0)

    @pl.when((n_blocks & 1) == 1)
    def _():
        step(qi, 0, True)
        values(qi, 0)

    @pl.when((n_blocks & 1) == 0)
    def _():
        values(qi, 1)

    lv = lam_ref[...]
    lam = (jnp.exp(jnp.sum(lv[0:1] * lv[1:2], axis=-1, keepdims=True))
           - jnp.exp(jnp.sum(lv[2:3] * lv[3:4], axis=-1, keepdims=True)) + lam_init)
    dv = out_ref.shape[1]
    o = (acc_ref[0, 0:dv, :] / acc_ref[0, dv:dv + 1, :]
         - lam * (acc_ref[1, 0:dv, :] / acc_ref[1, dv:dv + 1, :]))
    yn = o * lax.rsqrt(jnp.mean(o * o, axis=0, keepdims=True) + EPS) * ng_ref[...] * (1.0 - lam_init)
    out_ref[...] = yn.T.astype(BF16)


def _attention(qt, ak, vt, bias, lamv, ngb, *, batch, seq, nh, dv, lam_init):
    bq = ATT_BLOCK
    nq = seq // bq
    body = functools.partial(_attn_body, lam_init=lam_init)
    return pl.pallas_call(
        body,
        grid=(batch, nh, nq),
        in_specs=[pl.BlockSpec((dv, bq), lambda b, h, i: (h, b * nq + i)),
                  pl.BlockSpec((seq, dv), lambda b, h, i: (b, h)),
                  pl.BlockSpec((nq, dv + V_EXTRA, bq), lambda b, h, i: (b, h, 0)),
                  pl.BlockSpec((1, 3, bq, bq), lambda b, h, i: (h, 0, 0, 0)),
                  pl.BlockSpec(lamv.shape, lambda b, h, i: (0, 0)),
                  pl.BlockSpec(ngb.shape, lambda b, h, i: (0, 0))],
        out_specs=pl.BlockSpec((bq, dv), lambda b, h, i: (b * nq + i, h)),
        out_shape=jax.ShapeDtypeStruct((batch * seq, nh * dv), BF16),
        scratch_shapes=[pltpu.VMEM((2, dv + V_EXTRA, bq), F32), pltpu.VMEM((2, 8, bq), F32),
                        pltpu.VMEM((2, 2, bq, bq), F32), pltpu.VMEM((2, 2, bq, bq), BF16),
                        pltpu.VMEM((2, 2, 8, bq), F32)],
        compiler_params=_cparams(("arbitrary", "arbitrary", "arbitrary")),
        name="diffattn",
    )(qt, ak, vt, bias, lamv, ngb)


def _rel_bucket(rel):
    nb = N_BUCKETS // 2
    max_exact = nb // 2
    ret = (rel > 0).astype(jnp.int32) * nb
    n = jnp.abs(rel)
    nf = jnp.maximum(n, 1).astype(F32)
    large = max_exact + (jnp.log(nf / max_exact) / math.log(MAX_DISTANCE / max_exact)
                         * (nb - max_exact)).astype(jnp.int32)
    large = jnp.minimum(large, nb - 1)
    return ret + jnp.where(n < max_exact, n, large)


def _bias_tiles(rel_bias):
    b = ATT_BLOCK
    table = rel_bias.astype(F32)

    def lookup(rel):
        onehot = _rel_bucket(rel)[..., None] == jnp.arange(N_BUCKETS, dtype=jnp.int32)
        return jnp.sum(jnp.where(onehot[..., None], table, 0.0), axis=-2)

    kpos = jnp.arange(b, dtype=jnp.int32)[:, None]
    qpos = jnp.arange(b, dtype=jnp.int32)[None, :]
    far = lookup(jnp.array(-(2 * b), jnp.int32))
    prev = (lookup(kpos - b - qpos) - far) * LOG2E
    diag = (lookup(kpos - qpos) - far) * LOG2E
    diag = jnp.where((kpos // CHUNK <= qpos // CHUNK)[..., None], diag, NEG)
    return jnp.transpose(jnp.stack([jnp.zeros_like(prev), prev, diag], axis=0), (3, 0, 1, 2))


def _router_body(x_ref, mo_ref, ao_ref, wo_ref, g2_ref, wr_ref, br_ref, tril_ref,
                 h1_ref, u2_ref, idx_ref, gate_ref, rank_ref, cnt_ref, run_ref, *, mw):
    @pl.when(pl.program_id(0) == 0)
    def _():
        run_ref[...] = jnp.zeros_like(run_ref)

    h1 = x_ref[...] + _dot(mo_ref[...], wo_ref[0:mw, :]) + _dot(ao_ref[...], wo_ref[mw:, :])
    h1_ref[...] = h1
    u2 = h1 * lax.rsqrt(jnp.mean(h1 * h1, axis=-1, keepdims=True) + EPS) * g2_ref[...]
    for j in range(u2.shape[1] // LANES):
        u2_ref[pl.ds(j, u2.shape[0], stride=ROW_TILE), :] = u2[:, j * LANES:(j + 1) * LANES]
    logits = _dot(u2.astype(BF16), wr_ref[...]) + br_ref[...]
    lane = lax.broadcasted_iota(jnp.int32, logits.shape, 1).astype(F32)
    l = logits
    vals, sels, idxs = [], [], []
    for _ in range(TOP_K):
        mx = jnp.max(l, axis=-1, keepdims=True)
        ik = jnp.min(jnp.where(l == mx, lane, float(LANES)), axis=-1, keepdims=True)
        sel = lane == ik
        l = jnp.where(sel, -jnp.inf, l)
        vals.append(mx)
        idxs.append(ik)
        sels.append(sel)
    es = [jnp.exp(v - vals[0]) for v in vals]
    inv = 1.0 / sum(es)
    oh = sum(s.astype(F32) for s in sels)
    tot = _dot(tril_ref[...], oh.astype(BF16)) + run_ref[...]
    ranks = [jnp.sum(jnp.where(s, tot, 0.0), axis=-1, keepdims=True) for s in sels]
    run_ref[...] = run_ref[...] + jnp.sum(oh, axis=0, keepdims=True)
    cnt_ref[...] = run_ref[...]

    def pack(cols):
        out = jnp.zeros(logits.shape, F32)
        for j, cj in enumerate(cols):
            out = jnp.where(lane == float(j), cj, out)
        return out

    idx_ref[...] = pack(idxs).astype(jnp.int32)
    gate_ref[...] = pack([e * inv for e in es])
    rank_ref[...] = pack(ranks).astype(jnp.int32)


def _router(x2, m_out, a_out, wo, g2, wr, br, *, tm):
    t_, d = x2.shape
    mw = m_out.shape[1]
    nt = t_ // tm
    tril = (jnp.arange(tm)[:, None] > jnp.arange(tm)[None, :]).astype(BF16)
    body = functools.partial(_router_body, mw=mw)
    row = lambda w: pl.BlockSpec((tm, w), lambda i: (i, 0))
    full = lambda a: pl.BlockSpec(a.shape, lambda i: (0,) * a.ndim)
    return pl.pallas_call(
        body,
        grid=(nt,),
        in_specs=[row(d), row(mw), row(a_out.shape[1]), full(wo), full(g2), full(wr), full(br), full(tril)],
        out_specs=[row(d), pl.BlockSpec((tm * d // LANES, LANES), lambda i: (i, 0)),
                   row(LANES), row(LANES), row(LANES),
                   pl.BlockSpec((1, LANES), lambda i: (0, 0))],
        out_shape=[jax.ShapeDtypeStruct((t_, d), F32),
                   jax.ShapeDtypeStruct((t_ * d // LANES, LANES), F32),
                   jax.ShapeDtypeStruct((t_, LANES), jnp.int32),
                   jax.ShapeDtypeStruct((t_, LANES), F32),
                   jax.ShapeDtypeStruct((t_, LANES), jnp.int32),
                   jax.ShapeDtypeStruct((1, LANES), F32)],
        scratch_shapes=[pltpu.VMEM((1, LANES), F32)],
        compiler_params=_cparams(("arbitrary",)),
        name="router",
    )(x2, m_out, a_out, wo, g2, wr, br, tril)


def _pair_loop(n_tok, fn):
    def body(r, c):
        for k in range(TOP_K):
            fn(r, k)
        return c
    lax.fori_loop(0, n_tok, body, 0, unroll=4)


def _dispatch_body(plo_ref, phi_ref, nu_ref, pos_ref, u2_ref, xs_ref, zero_ref, sem, zsem, *, tmd):
    @pl.when(pl.program_id(0) == 0)
    def _():
        zero_ref[...] = jnp.zeros_like(zero_ref)
        r = zero_ref.shape[0]

        def zrow(row):
            return pltpu.make_async_copy(zero_ref.at[0], xs_ref.at[row], zsem)

        def zblock(blk):
            return pltpu.make_async_copy(zero_ref, xs_ref.at[pl.ds(pl.multiple_of(blk * r, r), r)], zsem)

        def fill(row_fn, blk_fn):
            def per_expert(e, c):
                return lax.fori_loop(plo_ref[e], phi_ref[e], lambda row, c2: (row_fn(row), c2)[1], c)
            lax.fori_loop(0, plo_ref.shape[0], per_expert, 0)
            lax.fori_loop(nu_ref[0], xs_ref.shape[0] // r, lambda blk, c: (blk_fn(blk), c)[1], 0)

        fill(lambda row: zrow(row).start(), lambda blk: zblock(blk).start())
        fill(lambda row: zrow(row).wait(), lambda blk: zblock(blk).wait())

    def copy(r, k):
        return pltpu.make_async_copy(u2_ref.at[r], xs_ref.at[pos_ref[0, 0, r * TOP_K + k]], sem)

    _pair_loop(tmd, lambda r, k: copy(r, k).start(priority=k % 2))
    _pair_loop(tmd, lambda r, k: copy(r, k).wait())


def _dispatch(pad_lo, pad_hi, n_used, pos, u2c, p_rows, *, tmd):
    t_ = u2c.shape[0]
    nt = t_ // tmd
    pos3 = pos.reshape(nt, 1, tmd * TOP_K)
    grid_spec = pltpu.PrefetchScalarGridSpec(
        num_scalar_prefetch=3,
        grid=(nt,),
        in_specs=[pl.BlockSpec((1, 1, tmd * TOP_K), lambda i, lo, hi, nu: (i, 0, 0), memory_space=pltpu.SMEM),
                  pl.BlockSpec((tmd,) + u2c.shape[1:], lambda i, lo, hi, nu: (i, 0, 0))],
        out_specs=pl.BlockSpec(memory_space=pl.ANY),
        scratch_shapes=[pltpu.VMEM((EXPERT_ROWS,) + u2c.shape[1:], F32),
                        pltpu.SemaphoreType.DMA, pltpu.SemaphoreType.DMA],
    )
    return pl.pallas_call(
        functools.partial(_dispatch_body, tmd=tmd),
        grid_spec=grid_spec,
        out_shape=jax.ShapeDtypeStruct((p_rows,) + u2c.shape[1:], F32),
        compiler_params=_cparams(("arbitrary",)),
        name="moe_dispatch",
    )(pad_lo, pad_hi, n_used, pos3, u2c)


def _experts_body(be_ref, nu_ref, xs_ref, wgu_ref, bgu_ref, wd_ref, bd_ref, ys_ref, wgu_bf, wd_bf, *, dff, r):
    i = pl.program_id(0)
    nj = wd_ref.shape[2] // LANES
    used = i < nu_ref[0]

    @pl.when(used & ((i == 0) | (be_ref[i] != be_ref[jnp.maximum(i - 1, 0)])))
    def _():
        wgu_bf[...] = wgu_ref[0].astype(BF16)
        wd_bf[...] = wd_ref[0].astype(BF16)

    @pl.when(used)
    def _():
        x = jnp.concatenate([xs_ref[pl.ds(j, r, stride=ROW_TILE), :] for j in range(nj)], axis=1)
        gu = _dot(x.astype(BF16), wgu_bf[...]) + bgu_ref[0]
        gate = jnp.minimum(gu[:, :dff], SWIGLU_LIMIT)
        up = jnp.clip(gu[:, dff:], -SWIGLU_LIMIT, SWIGLU_LIMIT)
        act = (up + 1.0) * gate * _sigmoid(SWIGLU_ALPHA * gate)
        y = _dot(act.astype(BF16), wd_bf[...]) + bd_ref[0]
        for j in range(nj):
            ys_ref[pl.ds(j, r, stride=ROW_TILE), :] = y[:, j * LANES:(j + 1) * LANES]

    @pl.when(jnp.logical_not(used))
    def _():
        ys_ref[...] = jnp.zeros_like(ys_ref)


def _experts(blk_e, n_used, xs2, wgu, bgu, wd, bd):
    r = EXPERT_ROWS
    dff, d = wd.shape[1], wd.shape[2]
    rt = r * d // LANES
    grid_spec = pltpu.PrefetchScalarGridSpec(
        num_scalar_prefetch=2,
        grid=(xs2.shape[0] // rt,),
        in_specs=[pl.BlockSpec((rt, LANES), lambda i, be, nu: (i, 0)),
                  pl.BlockSpec((1, d, 2 * dff), lambda i, be, nu: (be[i], 0, 0)),
                  pl.BlockSpec((1, 1, 2 * dff), lambda i, be, nu: (be[i], 0, 0)),
                  pl.BlockSpec((1, dff, d), lambda i, be, nu: (be[i], 0, 0)),
                  pl.BlockSpec((1, 1, d), lambda i, be, nu: (be[i], 0, 0))],
        out_specs=pl.BlockSpec((rt, LANES), lambda i, be, nu: (i, 0)),
        scratch_shapes=[pltpu.VMEM((d, 2 * dff), BF16), pltpu.VMEM((dff, d), BF16)],
    )
    return pl.pallas_call(
        functools.partial(_experts_body, dff=dff, r=r),
        grid_spec=grid_spec,
        out_shape=jax.ShapeDtypeStruct(xs2.shape, F32),
        compiler_params=_cparams(("arbitrary",)),
        name="moe_experts",
    )(blk_e, n_used, xs2, wgu, bgu, wd, bd)


def _combine_body(pos_ref, posn_ref, h1_ref, gate_ref, fg_ref, ys_ref, out_ref, rows_ref, sems,
                  *, tmc, final):
    i = pl.program_id(0)
    slot = i % 2

    def copy(pref, sl, r, k):
        return pltpu.make_async_copy(
            ys_ref.at[pref[0, 0, r * TOP_K + k]],
            rows_ref.at[sl, k, pl.ds(pl.multiple_of(r * ROW_TILE, ROW_TILE), ROW_TILE)],
            sems.at[sl])

    @pl.when(i == 0)
    def _():
        _pair_loop(tmc, lambda r, k: copy(pos_ref, 0, r, k).start(priority=k % 2))

    @pl.when(i + 1 < pl.num_programs(0))
    def _():
        _pair_loop(tmc, lambda r, k: copy(posn_ref, 1 - slot, r, k).start(priority=k % 2))

    _pair_loop(tmc, lambda r, k: copy(pos_ref, slot, r, k).wait())
    g = gate_ref[...]
    cols = []
    for j in range(h1_ref.shape[1] // LANES):
        hj = h1_ref[:, j * LANES:(j + 1) * LANES]
        for k in range(TOP_K):
            hj = hj + g[:, k:k + 1] * rows_ref[slot, k, pl.ds(j, tmc, stride=ROW_TILE), :]
        cols.append(hj)
    h = jnp.concatenate(cols, axis=1)
    if final:
        h = h * lax.rsqrt(jnp.mean(h * h, axis=-1, keepdims=True) + EPS) * fg_ref[...]
    out_ref[...] = h


def _combine(pos, h1, gates, fg, ys3, *, tmc, final):
    t_, d = h1.shape
    nt = t_ // tmc
    pos3 = pos.reshape(nt, 1, tmc * TOP_K)
    row = lambda w: pl.BlockSpec((tmc, w), lambda i: (i, 0))
    smem = lambda f: pl.BlockSpec((1, 1, tmc * TOP_K), f, memory_space=pltpu.SMEM)
    return pl.pallas_call(
        functools.partial(_combine_body, tmc=tmc, final=final),
        grid=(nt,),
        in_specs=[smem(lambda i: (i, 0, 0)), smem(lambda i: (jnp.minimum(i + 1, nt - 1), 0, 0)),
                  row(d), row(LANES), pl.BlockSpec(fg.shape, lambda i: (0, 0)),
                  pl.BlockSpec(memory_space=pl.ANY)],
        out_specs=row(d),
        out_shape=jax.ShapeDtypeStruct((t_, d), F32),
        scratch_shapes=[pltpu.VMEM((2, TOP_K, tmc * ROW_TILE, LANES), F32), pltpu.SemaphoreType.DMA((2,))],
        compiler_params=_cparams(("arbitrary",)),
        name="moe_combine",
    )(pos3, pos3, h1, gates, fg, ys3)


def _pick_tile(n, pref):
    t = pref
    while n % t:
        t //= 2
    return t


def kernel(x, rel_bias, norm1_g, w_in, conv_w, conv_b, gate_b, mlstm_norm_g, lam_q1, lam_k1, lam_q2,
           lam_k2, diff_norm_g, w_out, norm2_g, router_w, router_b, w_gu, b_gu, w_down, b_down, final_g):
    batch, seq, d = x.shape
    depth = w_in.shape[0]
    t_ = batch * seq
    nh = M_HEADS
    mw = conv_w.shape[-1] // 2
    dh = mw // nh
    aw = w_out.shape[1] - mw
    dv = aw // A_HEADS
    n_exp = router_w.shape[-1]
    dff = w_down.shape[-2]
    assert seq % ATT_BLOCK == 0 and seq % MLSTM_BLOCK == 0 and dh == LANES and dv == LANES and n_exp <= LANES
    assert d == ROW_TILE * LANES, 'token rows are moved as single (8, 128) f32 tiles'
    tm = _pick_tile(t_, 512)
    tb = MLSTM_BLOCK
    r = EXPERT_ROWS
    p_rows = t_ * TOP_K + n_exp * r

    bias_tiles = _bias_tiles(rel_bias)
    h = x.reshape(t_, d)
    for l in range(depth):
        w = w_in[l]
        c0 = 4 * mw
        c1 = c0 + 2 * nh
        zc = jnp.zeros((d, LANES - nh), F32)
        wa = jnp.concatenate([w[:, 0:2 * mw], w[:, 2 * mw:3 * mw], w[:, 3 * mw:c0],
                              w[:, c1 + aw:c1 + 2 * aw],
                              w[:, c0:c0 + nh], zc, w[:, c0 + nh:c1], zc], axis=1).astype(BF16)
        z4 = jnp.zeros((d, 8 - nh), F32)
        wt = jnp.concatenate([w[:, c1:c1 + aw] * ((dv // 2) ** -0.5 * LOG2E), w[:, c1 + 2 * aw:c1 + 3 * aw],
                              w[:, c0:c0 + nh], z4, w[:, c0 + nh:c1], z4], axis=1).T.astype(BF16)
        gb = gate_b[l].astype(F32)
        zl = jnp.zeros((LANES - nh,), F32)
        gbr = jnp.concatenate([gb[:nh], zl, gb[nh:], zl])[None, :]
        z8 = jnp.zeros((8 - nh,), F32)
        gbc = jnp.broadcast_to(jnp.concatenate([gb[:nh], z8, gb[nh:], z8])[:, None], (16, tm))
        mqk, mv, mo, ak, gc, qt, vt, gt = _inproj(h, norm1_g[l][None, :], wa, wt, gbr, gbc,
                                                  mw=mw, aw=aw, tm=tm)
        m_out = _mlstm(mqk, mv, mo, gc, gt, conv_w[l], conv_b[l][None, :],
                       mlstm_norm_g[l].reshape(nh, dh), batch=batch, seq=seq, nh=nh, dh=dh, tb=tb)
        lam_init = 0.8 - 0.6 * math.exp(-0.3 * l)
        lamv = jnp.stack([lam_q1[l], lam_k1[l], lam_q2[l], lam_k2[l]]).astype(F32)
        ngb = jnp.broadcast_to(diff_norm_g[l].astype(F32)[:, None], (dv, ATT_BLOCK))
        a_out = _attention(qt, ak, vt, bias_tiles, lamv, ngb, batch=batch, seq=seq, nh=A_HEADS, dv=dv,
                           lam_init=lam_init)
        wr = jnp.concatenate([router_w[l], jnp.zeros((d, LANES - n_exp), F32)], axis=1).astype(BF16)
        br = jnp.concatenate([router_b[l].astype(F32), jnp.full((LANES - n_exp,), NEG, F32)])[None, :]
        h1, u2c, idx, gates, rank, cnt = _router(h, m_out, a_out, w_out[l].astype(BF16), norm2_g[l][None, :],
                                                 wr, br, tm=tm)
        counts = cnt[0, :n_exp].astype(jnp.int32)
        padded = ((counts + r - 1) // r) * r
        pends = jnp.cumsum(padded)
        pstarts = pends - padded
        onehot = idx[:, :TOP_K, None] == jnp.arange(n_exp, dtype=jnp.int32)
        pos = (jnp.sum(jnp.where(onehot, pstarts, 0), axis=-1) + rank[:, :TOP_K]).reshape(-1)
        blk_start = jnp.arange(p_rows // r, dtype=jnp.int32) * r
        blk_e = jnp.minimum(jnp.sum(pends[None, :] <= blk_start[:, None], axis=1), n_exp - 1).astype(jnp.int32)
        n_used = (pends[-1:] // r).astype(jnp.int32)
        nrt = d // LANES
        xs3 = _dispatch(pstarts + counts, pends, n_used, pos, u2c.reshape(t_, nrt, LANES), p_rows,
                        tmd=_pick_tile(t_, 256))
        ys2 = _experts(blk_e, n_used, xs3.reshape(p_rows * nrt, LANES), w_gu[l].astype(F32),
                       b_gu[l][:, None, :].astype(F32), w_down[l].astype(F32), b_down[l][:, None, :].astype(F32))
        h = _combine(pos, h1, gates, final_g[None, :].astype(F32), ys2.reshape(p_rows, nrt, LANES),
                     tmc=_pick_tile(t_, 256), final=(l == depth - 1))
    return h.reshape(batch, seq, d)
```

```python
import functools
import math

import jax
import jax.numpy as jnp
from jax import lax
from jax.experimental import pallas as pl
from jax.experimental.pallas import tpu as pltpu

F32 = jnp.float32
BF16 = jnp.bfloat16
EPS = 1e-5
NEG = -1e30
LOG2E = math.log2(math.e)

CHUNK = 64
M_HEADS = 4
A_HEADS = 4
CONV_K = 4
N_BUCKETS = 32
MAX_DISTANCE = 128
TOP_K = 4
SWIGLU_LIMIT = 7.0
SWIGLU_ALPHA = 1.702

LANES = 128
ROW_TILE = 8
ATT_BLOCK = 512
MLSTM_BLOCK = 256
V_EXTRA = 16
EXPERT_ROWS = 256
VMEM_LIMIT = 56 * 1024 * 1024


def _cparams(sem):
    return pltpu.CompilerParams(dimension_semantics=sem, vmem_limit_bytes=VMEM_LIMIT)


def _dot(a, b):
    return jnp.dot(a, b, preferred_element_type=F32)


def _dot_nt(a, b):
    return lax.dot_general(a, b, (((1,), (1,)), ((), ())), preferred_element_type=F32)


def _log_sigmoid(x):
    return jnp.minimum(x, 0.0) - jnp.log(1.0 + jnp.exp(-jnp.abs(x)))


def _sigmoid(x):
    return 1.0 / (1.0 + jnp.exp(-x))


def _inproj_body(x_ref, g_ref, wa_ref, wt_ref, gbr_ref, gbc_ref,
                 mqk_ref, mv_ref, mo_ref, ak_ref, gc_ref, qt_ref, vt_ref, gt_ref,
                 *, mw, aw, tm):
    x = x_ref[...]
    u = x * lax.rsqrt(jnp.mean(x * x, axis=-1, keepdims=True) + EPS) * g_ref[...]
    ub = u.astype(BF16)

    def mm(a, b):
        return _dot(ub, wa_ref[:, a:b])

    c0 = 2 * mw
    mqk_ref[...] = mm(0, c0).astype(BF16)
    mv_ref[...] = mm(c0, c0 + mw).astype(BF16)
    mo_ref[...] = mm(c0 + mw, c0 + 2 * mw).astype(BF16)
    c1 = c0 + 2 * mw
    ak_ref[...] = mm(c1, c1 + aw).astype(BF16)
    gc_ref[...] = mm(c1 + aw, c1 + aw + 2 * LANES) + gbr_ref[...]
    t = _dot_nt(wt_ref[...], ub)
    qt_ref[...] = t[0:aw].astype(BF16)
    dv = aw // A_HEADS
    ones_rows = (lax.broadcasted_iota(jnp.int32, (V_EXTRA, ATT_BLOCK), 0) == 0).astype(BF16)
    for j in range(tm // ATT_BLOCK):
        for h in range(A_HEADS):
            r0 = h * (dv + V_EXTRA)
            vt_ref[j, r0:r0 + dv, :] = t[aw + h * dv:aw + (h + 1) * dv,
                                         j * ATT_BLOCK:(j + 1) * ATT_BLOCK].astype(BF16)
            vt_ref[j, r0 + dv:r0 + dv + V_EXTRA, :] = ones_rows
    gt_ref[...] = t[2 * aw:2 * aw + 16] + gbc_ref[...]


def _inproj(x2, g1, wa, wt, gbr, gbc, *, mw, aw, tm):
    t_, d = x2.shape
    nt = t_ // tm
    body = functools.partial(_inproj_body, mw=mw, aw=aw, tm=tm)
    row = lambda w: pl.BlockSpec((tm, w), lambda i: (i, 0))
    full = lambda a: pl.BlockSpec(a.shape, lambda i: (0,) * a.ndim)
    return pl.pallas_call(
        body,
        grid=(nt,),
        in_specs=[row(d), full(g1), full(wa), full(wt), full(gbr), full(gbc)],
        out_specs=[row(2 * mw), row(mw), row(mw), row(aw), row(2 * LANES),
                   pl.BlockSpec((aw, tm), lambda i: (0, i)),
                   pl.BlockSpec((tm // ATT_BLOCK, aw + A_HEADS * V_EXTRA, ATT_BLOCK), lambda i: (i, 0, 0)),
                   pl.BlockSpec((16, tm), lambda i: (0, i))],
        out_shape=[jax.ShapeDtypeStruct((t_, 2 * mw), BF16),
                   jax.ShapeDtypeStruct((t_, mw), BF16),
                   jax.ShapeDtypeStruct((t_, mw), BF16),
                   jax.ShapeDtypeStruct((t_, aw), BF16),
                   jax.ShapeDtypeStruct((t_, 2 * LANES), F32),
                   jax.ShapeDtypeStruct((aw, t_), BF16),
                   jax.ShapeDtypeStruct((t_ // ATT_BLOCK, aw + A_HEADS * V_EXTRA, ATT_BLOCK), BF16),
                   jax.ShapeDtypeStruct((16, t_), F32)],
        compiler_params=_cparams(("arbitrary",)),
        name="inproj",
    )(x2, g1, wa, wt, gbr, gbc)


def _segment_scan(x, pos, axis, op, identity):
    d = 1
    while d < CHUNK:
        sh = pltpu.roll(x, d, axis=axis)
        x = op(x, jnp.where(pos >= d, sh, identity))
        d *= 2
    return x


def _mlstm_body(mqk_ref, mv_ref, mo_ref, gc_ref, gt_ref, cw_ref, cb_ref, ng_ref, out_ref,
                ext_ref, qk_ref, kt_ref, gs_ref, cst_ref, mst_ref, *, tb, nh, dh):
    mw = nh * dh

    @pl.when(pl.program_id(1) == 0)
    def _():
        ext_ref[0:8, :] = jnp.zeros((8, 2 * mw), F32)
        cst_ref[...] = jnp.zeros_like(cst_ref)
        mst_ref[...] = jnp.zeros_like(mst_ref)

    ext_ref[8:8 + tb, :] = mqk_ref[...].astype(F32)
    y = cb_ref[...]
    for j in range(CONV_K):
        y = y + cw_ref[j:j + 1, :] * ext_ref[8 - (CONV_K - 1) + j:8 - (CONV_K - 1) + j + tb, :]
    ext_ref[0:8, :] = ext_ref[tb:tb + 8, :]
    act = y * _sigmoid(y)
    qk_ref[:, 0:mw] = act[:, 0:mw].astype(BF16)
    k_all = act[:, mw:2 * mw] * (dh ** -0.5)
    qk_ref[:, mw:2 * mw] = k_all.astype(BF16)
    kt_ref[...] = k_all.T

    gc = gc_ref[...]
    rpos = lax.broadcasted_iota(jnp.int32, (tb, LANES), 0) & (CHUNK - 1)
    bc = _segment_scan(_log_sigmoid(gc[:, LANES:]), rpos, 0, jnp.add, 0.0)
    a_c = gc[:, :LANES] - bc
    pm = _segment_scan(a_c, rpos, 0, jnp.maximum, NEG)
    gs_ref[0] = bc
    gs_ref[1] = a_c
    gs_ref[2] = pm
    gt = gt_ref[...]
    lpos = lax.broadcasted_iota(jnp.int32, (8, tb), 1) & (CHUNK - 1)
    a_r = gt[0:8] - _segment_scan(_log_sigmoid(gt[8:16]), lpos, 1, jnp.add, 0.0)

    tri = (lax.broadcasted_iota(jnp.int32, (CHUNK, CHUNK), 0)
           >= lax.broadcasted_iota(jnp.int32, (CHUNK, CHUNK), 1))
    ones_col = (lax.broadcasted_iota(jnp.int32, (CHUNK, dh), 1) == 0).astype(BF16)
    ones_grp = (lax.broadcasted_iota(jnp.int32, (LANES, dh), 1) == 0).astype(BF16)

    m_state = [mst_ref[h, 0:1, 0:1] for h in range(nh)]
    for c in range(tb // CHUNK):
        r0 = c * CHUNK
        for h in range(nh):
            q = qk_ref[r0:r0 + CHUNK, h * dh:(h + 1) * dh]
            k = qk_ref[r0:r0 + CHUNK, mw + h * dh:mw + (h + 1) * dh]
            vext = jnp.concatenate([mv_ref[r0:r0 + CHUNK, h * dh:(h + 1) * dh], ones_col], axis=1)
            m_prev = m_state[h]
            bc_h = gs_ref[0, r0:r0 + CHUNK, h:h + 1]
            ac_h = gs_ref[1, r0:r0 + CHUNK, h:h + 1]
            pm_h = gs_ref[2, r0:r0 + CHUNK, h:h + 1]
            big_m = jnp.maximum(pm_h, m_prev)
            e = jnp.where(tri, jnp.exp(a_r[h:h + 1, r0:r0 + CHUNK] - big_m), 0.0)
            w = (e * _dot_nt(q, k)).astype(BF16)
            cext = cst_ref[h]
            r = _dot(w, vext) + jnp.exp(m_prev - big_m) * _dot(q, cext.astype(BF16))
            num = r[:, 0:dh]
            den = r[:, dh:dh + 1]
            hh = num / jnp.maximum(jnp.abs(den), jnp.exp(-(bc_h + big_m)))
            yn = hh * lax.rsqrt(jnp.mean(hh * hh, axis=-1, keepdims=True) + EPS) * ng_ref[h:h + 1, :]
            og = _sigmoid(mo_ref[r0:r0 + CHUNK, h * dh:(h + 1) * dh].astype(F32))
            out_ref[r0:r0 + CHUNK, h * dh:(h + 1) * dh] = (og * yn).astype(BF16)
            m_last = jnp.maximum(m_prev, pm_h[CHUNK - 1:CHUNK, :])
            g0 = (r0 // LANES) * LANES
            in_chunk = (lax.broadcasted_iota(jnp.int32, (1, LANES), 1) // CHUNK) == (r0 - g0) // CHUNK
            wrow = jnp.where(in_chunk, jnp.exp(a_r[h:h + 1, g0:g0 + LANES] - m_last), 0.0)
            wkt = (kt_ref[h * dh:(h + 1) * dh, g0:g0 + LANES] * wrow).astype(BF16)
            vgrp = jnp.concatenate([mv_ref[g0:g0 + LANES, h * dh:(h + 1) * dh], ones_grp], axis=1)
            cst_ref[h] = jnp.exp(m_prev - m_last) * cext + _dot(wkt, vgrp)
            m_state[h] = bc_h[CHUNK - 1:CHUNK, :] + m_last
    for h in range(nh):
        mst_ref[h] = jnp.broadcast_to(m_state[h], mst_ref.shape[1:])


def _mlstm(mqk, mv, mo, gc, gt, cw, cb, ng, *, batch, seq, nh, dh, tb):
    mw = nh * dh
    nb = seq // tb
    body = functools.partial(_mlstm_body, tb=tb, nh=nh, dh=dh)
    row = lambda w: pl.BlockSpec((tb, w), lambda b, j: (b * nb + j, 0))
    full = lambda a: pl.BlockSpec(a.shape, lambda b, j: (0,) * a.ndim)
    return pl.pallas_call(
        body,
        grid=(batch, nb),
        in_specs=[row(2 * mw), row(mw), row(mw), row(2 * LANES),
                  pl.BlockSpec((16, tb), lambda b, j: (0, b * nb + j)),
                  full(cw), full(cb), full(ng)],
        out_specs=row(mw),
        out_shape=jax.ShapeDtypeStruct((batch * seq, mw), BF16),
        scratch_shapes=[pltpu.VMEM((tb + 8, 2 * mw), F32),
                        pltpu.VMEM((tb, 2 * mw), BF16),
                        pltpu.VMEM((mw, tb), F32),
                        pltpu.VMEM((3, tb, LANES), F32),
                        pltpu.VMEM((nh, dh, 2 * dh), F32),
                        pltpu.VMEM((nh, 8, LANES), F32)],
        compiler_params=_cparams(("arbitrary", "arbitrary")),
        name="mlstm",
    )(mqk, mv, mo, gc, gt, cw, cb, ng)


def _attn_body(qt_ref, k_ref, vt_ref, bias_ref, lam_ref, ng_ref, out_ref,
               acc_ref, ml_ref, s_ref, p_ref, al_ref, *, lam_init):
    bq = ATT_BLOCK
    qi = pl.program_id(2)
    qt = qt_ref[...]
    half = qt.shape[0] // 2
    rowi = lax.broadcasted_iota(jnp.int32, qt.shape, 0)
    zero = jnp.zeros_like(qt)
    qs = (jnp.where(rowi < half, qt, zero), jnp.where(rowi >= half, qt, zero))
    acc_ref[...] = jnp.zeros_like(acc_ref)
    ml_ref[...] = jnp.full(ml_ref.shape, NEG, F32)

    def scores(j, slot, biased):
        kb = k_ref[pl.ds(pl.multiple_of(j * bq, bq), bq), :]
        for m in range(2):
            s = _dot(kb, qs[m])
            if biased:
                s = s + bias_ref[0, jnp.clip(j - (qi - 2), 0, 2)]
            s_ref[slot, m] = s

    def softmax(slot):
        for m in range(2):
            s = s_ref[slot, m]
            m_old = ml_ref[m, 0:1, :]
            m_new = jnp.maximum(m_old, jnp.max(s, axis=0, keepdims=True))
            ml_ref[m, 0:1, :] = m_new
            p_ref[slot, m] = jnp.exp2(s - m_new).astype(BF16)
            al_ref[slot, m, 0:1, :] = jnp.exp2(m_old - m_new)

    def values(j, slot):
        vb = vt_ref[j]
        for m in range(2):
            acc_ref[m] = al_ref[slot, m, 0:1, :] * acc_ref[m] + _dot(vb, p_ref[slot, m])

    def step(i, cur, biased):
        scores(jnp.minimum(i + 1, qi), 1 - cur, biased)
        softmax(cur)
        values(i - 1, 1 - cur)

    def far_pair(t, carry):
        step(2 * t + 1, 1, False)
        step(2 * t + 2, 0, False)
        return carry

    def near_pair(t, carry):
        step(2 * t + 1, 1, True)
        step(2 * t + 2, 0, True)
        return carry

    n_blocks = qi + 1
    n_pairs = lax.shift_right_logical(n_blocks - 1, 1)
    n_far = lax.shift_right_logical(jnp.maximum(n_blocks - 4, 0), 1)
    scores(0, 0, True)
    scores(jnp.minimum(1, qi), 1, True)
    softmax(0)
    lax.fori_loop(0, n_far, far_pair, 0)
    lax.fori_loop(n_far, n_pairs, near_pair, 0)

    @pl.when((n_blocks & 1) == 0)
    def _():
        step(qi, 1, True)
        values(qi, 1)

    @pl.when((n_blocks & 1) == 1)
    def _():
        values(qi, 0)

    lv = lam_ref[...]
    lam = (jnp.exp(jnp.sum(lv[0:1] * lv[1:2], axis=-1, keepdims=True))
           - jnp.exp(jnp.sum(lv[2:3] * lv[3:4], axis=-1, keepdims=True)) + lam_init)
    dv = out_ref.shape[1]
    o = (acc_ref[0, 0:dv, :] / acc_ref[0, dv:dv + 1, :]
         - lam * (acc_ref[1, 0:dv, :] / acc_ref[1, dv:dv + 1, :]))
    yn = o * lax.rsqrt(jnp.mean(o * o, axis=0, keepdims=True) + EPS) * ng_ref[...] * (1.0 - lam_init)
    out_ref[...] = yn.T.astype(BF16)


def _attention(qt, ak, vt, bias, lamv, ngb, *, batch, seq, nh, dv, lam_init):
    bq = ATT_BLOCK
    nq = seq // bq
    body = functools.partial(_attn_body, lam_init=lam_init)
    return pl.pallas_call(
        body,
        grid=(batch, nh, nq),
        in_specs=[pl.BlockSpec((dv, bq), lambda b, h, i: (h, b * nq + i)),
                  pl.BlockSpec((seq, dv), lambda b, h, i: (b, h)),
                  pl.BlockSpec((nq, dv + V_EXTRA, bq), lambda b, h, i: (b, h, 0)),
                  pl.BlockSpec((1, 3, bq, bq), lambda b, h, i: (h, 0, 0, 0)),
                  pl.BlockSpec(lamv.shape, lambda b, h, i: (0, 0)),
                  pl.BlockSpec(ngb.shape, lambda b, h, i: (0, 0))],
        out_specs=pl.BlockSpec((bq, dv), lambda b, h, i: (b * nq + i, h)),
        out_shape=jax.ShapeDtypeStruct((batch * seq, nh * dv), BF16),
        scratch_shapes=[pltpu.VMEM((2, dv + V_EXTRA, bq), F32), pltpu.VMEM((2, 8, bq), F32),
                        pltpu.VMEM((2, 2, bq, bq), F32), pltpu.VMEM((2, 2, bq, bq), BF16),
                        pltpu.VMEM((2, 2, 8, bq), F32)],
        compiler_params=_cparams(("arbitrary", "arbitrary", "arbitrary")),
        name="diffattn",
    )(qt, ak, vt, bias, lamv, ngb)


def _rel_bucket(rel):
    nb = N_BUCKETS // 2
    max_exact = nb // 2
    ret = (rel > 0).astype(jnp.int32) * nb
    n = jnp.abs(rel)
    nf = jnp.maximum(n, 1).astype(F32)
    large = max_exact + (jnp.log(nf / max_exact) / math.log(MAX_DISTANCE / max_exact)
                         * (nb - max_exact)).astype(jnp.int32)
    large = jnp.minimum(large, nb - 1)
    return ret + jnp.where(n < max_exact, n, large)


def _bias_tiles(rel_bias):
    b = ATT_BLOCK
    span = 2 * b
    table = rel_bias.astype(F32)

    def lookup(rel):
        onehot = _rel_bucket(rel)[..., None] == jnp.arange(N_BUCKETS, dtype=jnp.int32)
        return jnp.sum(jnp.where(onehot[..., None], table, 0.0), axis=-2)

    def toeplitz(w):
        flat = jnp.tile(w, (b, 1))[:b * (span - 1)]
        return flat.reshape(b, span - 1, -1)[:, :b]

    j = jnp.arange(span, dtype=jnp.int32)
    far = lookup(jnp.array(-span, jnp.int32))
    diag = toeplitz(lookup(jnp.where(j < b, -j, span - j)) - far) * LOG2E
    prev = toeplitz(lookup(jnp.where(j < b, -(j + b), b - j)) - far) * LOG2E
    kpos = jnp.arange(b, dtype=jnp.int32)[:, None]
    qpos = jnp.arange(b, dtype=jnp.int32)[None, :]
    diag = jnp.where((kpos // CHUNK <= qpos // CHUNK)[..., None], diag, NEG)
    return jnp.transpose(jnp.stack([jnp.zeros_like(prev), prev, diag], axis=0), (3, 0, 1, 2))


def _router_body(x_ref, mo_ref, ao_ref, wo_ref, g2_ref, wr_ref, br_ref, tril_ref,
                 h1_ref, u2_ref, idx_ref, gate_ref, rank_ref, cnt_ref, run_ref, *, mw):
    @pl.when(pl.program_id(0) == 0)
    def _():
        run_ref[...] = jnp.zeros_like(run_ref)

    h1 = x_ref[...] + _dot(mo_ref[...], wo_ref[0:mw, :]) + _dot(ao_ref[...], wo_ref[mw:, :])
    h1_ref[...] = h1
    u2 = h1 * lax.rsqrt(jnp.mean(h1 * h1, axis=-1, keepdims=True) + EPS) * g2_ref[...]
    for j in range(u2.shape[1] // LANES):
        u2_ref[pl.ds(j, u2.shape[0], stride=ROW_TILE), :] = u2[:, j * LANES:(j + 1) * LANES]
    logits = _dot(u2.astype(BF16), wr_ref[...]) + br_ref[...]
    lane = lax.broadcasted_iota(jnp.int32, logits.shape, 1).astype(F32)
    l = logits
    vals, sels, idxs = [], [], []
    for _ in range(TOP_K):
        mx = jnp.max(l, axis=-1, keepdims=True)
        ik = jnp.min(jnp.where(l == mx, lane, float(LANES)), axis=-1, keepdims=True)
        sel = lane == ik
        l = jnp.where(sel, -jnp.inf, l)
        vals.append(mx)
        idxs.append(ik)
        sels.append(sel)
    es = [jnp.exp(v - vals[0]) for v in vals]
    inv = 1.0 / sum(es)
    oh = sum(s.astype(F32) for s in sels)
    tot = _dot(tril_ref[...], oh.astype(BF16)) + run_ref[...]
    ranks = [jnp.sum(jnp.where(s, tot, 0.0), axis=-1, keepdims=True) for s in sels]
    run_ref[...] = run_ref[...] + jnp.sum(oh, axis=0, keepdims=True)
    cnt_ref[...] = run_ref[...]

    def pack(cols):
        out = jnp.zeros(logits.shape, F32)
        for j, cj in enumerate(cols):
            out = jnp.where(lane == float(j), cj, out)
        return out

    idx_ref[...] = pack(idxs).astype(jnp.int32)
    gate_ref[...] = pack([e * inv for e in es])
    rank_ref[...] = pack(ranks).astype(jnp.int32)


def _router(x2, m_out, a_out, wo, g2, wr, br, *, tm):
    t_, d = x2.shape
    mw = m_out.shape[1]
    nt = t_ // tm
    tril = (jnp.arange(tm)[:, None] > jnp.arange(tm)[None, :]).astype(BF16)
    body = functools.partial(_router_body, mw=mw)
    row = lambda w: pl.BlockSpec((tm, w), lambda i: (i, 0))
    full = lambda a: pl.BlockSpec(a.shape, lambda i: (0,) * a.ndim)
    return pl.pallas_call(
        body,
        grid=(nt,),
        in_specs=[row(d), row(mw), row(a_out.shape[1]), full(wo), full(g2), full(wr), full(br), full(tril)],
        out_specs=[row(d), pl.BlockSpec((tm * d // LANES, LANES), lambda i: (i, 0)),
                   row(LANES), row(LANES), row(LANES),
                   pl.BlockSpec((1, LANES), lambda i: (0, 0))],
        out_shape=[jax.ShapeDtypeStruct((t_, d), F32),
                   jax.ShapeDtypeStruct((t_ * d // LANES, LANES), F32),
                   jax.ShapeDtypeStruct((t_, LANES), jnp.int32),
                   jax.ShapeDtypeStruct((t_, LANES), F32),
                   jax.ShapeDtypeStruct((t_, LANES), jnp.int32),
                   jax.ShapeDtypeStruct((1, LANES), F32)],
        scratch_shapes=[pltpu.VMEM((1, LANES), F32)],
        compiler_params=_cparams(("arbitrary",)),
        name="router",
    )(x2, m_out, a_out, wo, g2, wr, br, tril)


def _pair_loop(n_tok, fn):
    def body(r, c):
        for k in range(TOP_K):
            fn(r, k)
        return c
    lax.fori_loop(0, n_tok, body, 0, unroll=4)


def _dispatch_body(plo_ref, phi_ref, nu_ref, pos_ref, u2_ref, xs_ref, zero_ref, sem, zsem, *, tmd):
    @pl.when(pl.program_id(0) == 0)
    def _():
        zero_ref[...] = jnp.zeros_like(zero_ref)
        r = zero_ref.shape[0]

        def zrow(row):
            return pltpu.make_async_copy(zero_ref.at[0], xs_ref.at[row], zsem)

        def zblock(blk):
            return pltpu.make_async_copy(zero_ref, xs_ref.at[pl.ds(pl.multiple_of(blk * r, r), r)], zsem)

        def fill(row_fn, blk_fn):
            def per_expert(e, c):
                return lax.fori_loop(plo_ref[e], phi_ref[e], lambda row, c2: (row_fn(row), c2)[1], c)
            lax.fori_loop(0, plo_ref.shape[0], per_expert, 0)
            lax.fori_loop(nu_ref[0], xs_ref.shape[0] // r, lambda blk, c: (blk_fn(blk), c)[1], 0)

        fill(lambda row: zrow(row).start(), lambda blk: zblock(blk).start())
        fill(lambda row: zrow(row).wait(), lambda blk: zblock(blk).wait())

    def copy(r, k):
        return pltpu.make_async_copy(u2_ref.at[r], xs_ref.at[pos_ref[0, 0, r * TOP_K + k]], sem)

    _pair_loop(tmd, lambda r, k: copy(r, k).start(priority=k % 2))
    _pair_loop(tmd, lambda r, k: copy(r, k).wait())


def _dispatch(pad_lo, pad_hi, n_used, pos, u2c, p_rows, *, tmd):
    t_ = u2c.shape[0]
    nt = t_ // tmd
    pos3 = pos.reshape(nt, 1, tmd * TOP_K)
    grid_spec = pltpu.PrefetchScalarGridSpec(
        num_scalar_prefetch=3,
        grid=(nt,),
        in_specs=[pl.BlockSpec((1, 1, tmd * TOP_K), lambda i, lo, hi, nu: (i, 0, 0), memory_space=pltpu.SMEM),
                  pl.BlockSpec((tmd,) + u2c.shape[1:], lambda i, lo, hi, nu: (i, 0, 0))],
        out_specs=pl.BlockSpec(memory_space=pl.ANY),
        scratch_shapes=[pltpu.VMEM((EXPERT_ROWS,) + u2c.shape[1:], F32),
                        pltpu.SemaphoreType.DMA, pltpu.SemaphoreType.DMA],
    )
    return pl.pallas_call(
        functools.partial(_dispatch_body, tmd=tmd),
        grid_spec=grid_spec,
        out_shape=jax.ShapeDtypeStruct((p_rows,) + u2c.shape[1:], F32),
        compiler_params=_cparams(("arbitrary",)),
        name="moe_dispatch",
    )(pad_lo, pad_hi, n_used, pos3, u2c)


def _experts_body(be_ref, nu_ref, xs_ref, wgu_ref, bgu_ref, wd_ref, bd_ref, ys_ref, wgu_bf, wd_bf, *, dff, r):
    i = pl.program_id(0)
    nj = wd_ref.shape[2] // LANES
    used = i < nu_ref[0]

    @pl.when(used & ((i == 0) | (be_ref[i] != be_ref[jnp.maximum(i - 1, 0)])))
    def _():
        wgu_bf[...] = wgu_ref[0].astype(BF16)
        wd_bf[...] = wd_ref[0].astype(BF16)

    @pl.when(used)
    def _():
        x = jnp.concatenate([xs_ref[pl.ds(j, r, stride=ROW_TILE), :] for j in range(nj)], axis=1)
        gu = _dot(x.astype(BF16), wgu_bf[...]) + bgu_ref[0]
        gate = jnp.minimum(gu[:, :dff], SWIGLU_LIMIT)
        up = jnp.clip(gu[:, dff:], -SWIGLU_LIMIT, SWIGLU_LIMIT)
        act = (up + 1.0) * gate * _sigmoid(SWIGLU_ALPHA * gate)
        y = _dot(act.astype(BF16), wd_bf[...]) + bd_ref[0]
        for j in range(nj):
            ys_ref[pl.ds(j, r, stride=ROW_TILE), :] = y[:, j * LANES:(j + 1) * LANES]

    @pl.when(jnp.logical_not(used))
    def _():
        ys_ref[...] = jnp.zeros_like(ys_ref)


def _experts(blk_e, n_used, xs2, wgu, bgu, wd, bd):
    r = EXPERT_ROWS
    dff, d = wd.shape[1], wd.shape[2]
    rt = r * d // LANES
    grid_spec = pltpu.PrefetchScalarGridSpec(
        num_scalar_prefetch=2,
        grid=(xs2.shape[0] // rt,),
        in_specs=[pl.BlockSpec((rt, LANES), lambda i, be, nu: (i, 0)),
                  pl.BlockSpec((1, d, 2 * dff), lambda i, be, nu: (be[i], 0, 0)),
                  pl.BlockSpec((1, 1, 2 * dff), lambda i, be, nu: (be[i], 0, 0)),
                  pl.BlockSpec((1, dff, d), lambda i, be, nu: (be[i], 0, 0)),
                  pl.BlockSpec((1, 1, d), lambda i, be, nu: (be[i], 0, 0))],
        out_specs=pl.BlockSpec((rt, LANES), lambda i, be, nu: (i, 0)),
        scratch_shapes=[pltpu.VMEM((d, 2 * dff), BF16), pltpu.VMEM((dff, d), BF16)],
    )
    return pl.pallas_call(
        functools.partial(_experts_body, dff=dff, r=r),
        grid_spec=grid_spec,
        out_shape=jax.ShapeDtypeStruct(xs2.shape, F32),
        compiler_params=_cparams(("arbitrary",)),
        name="moe_experts",
    )(blk_e, n_used, xs2, wgu, bgu, wd, bd)


def _combine_body(pos_ref, posn_ref, h1_ref, gate_ref, fg_ref, ys_ref, out_ref, rows_ref, sems,
                  *, tmc, final):
    i = pl.program_id(0)
    slot = i % 2

    def copy(pref, sl, r, k):
        return pltpu.make_async_copy(
            ys_ref.at[pref[0, 0, r * TOP_K + k]],
            rows_ref.at[sl, k, pl.ds(pl.multiple_of(r * ROW_TILE, ROW_TILE), ROW_TILE)],
            sems.at[sl])

    @pl.when(i == 0)
    def _():
        _pair_loop(tmc, lambda r, k: copy(pos_ref, 0, r, k).start(priority=k % 2))

    @pl.when(i + 1 < pl.num_programs(0))
    def _():
        _pair_loop(tmc, lambda r, k: copy(posn_ref, 1 - slot, r, k).start(priority=k % 2))

    _pair_loop(tmc, lambda r, k: copy(pos_ref, slot, r, k).wait())
    g = gate_ref[...]
    cols = []
    for j in range(h1_ref.shape[1] // LANES):
        hj = h1_ref[:, j * LANES:(j + 1) * LANES]
        for k in range(TOP_K):
            hj = hj + g[:, k:k + 1] * rows_ref[slot, k, pl.ds(j, tmc, stride=ROW_TILE), :]
        cols.append(hj)
    h = jnp.concatenate(cols, axis=1)
    if final:
        h = h * lax.rsqrt(jnp.mean(h * h, axis=-1, keepdims=True) + EPS) * fg_ref[...]
    out_ref[...] = h


def _combine(pos, h1, gates, fg, ys3, *, tmc, final):
    t_, d = h1.shape
    nt = t_ // tmc
    pos3 = pos.reshape(nt, 1, tmc * TOP_K)
    row = lambda w: pl.BlockSpec((tmc, w), lambda i: (i, 0))
    smem = lambda f: pl.BlockSpec((1, 1, tmc * TOP_K), f, memory_space=pltpu.SMEM)
    return pl.pallas_call(
        functools.partial(_combine_body, tmc=tmc, final=final),
        grid=(nt,),
        in_specs=[smem(lambda i: (i, 0, 0)), smem(lambda i: (jnp.minimum(i + 1, nt - 1), 0, 0)),
                  row(d), row(LANES), pl.BlockSpec(fg.shape, lambda i: (0, 0)),
                  pl.BlockSpec(memory_space=pl.ANY)],
        out_specs=row(d),
        out_shape=jax.ShapeDtypeStruct((t_, d), F32),
        scratch_shapes=[pltpu.VMEM((2, TOP_K, tmc * ROW_TILE, LANES), F32), pltpu.SemaphoreType.DMA((2,))],
        compiler_params=_cparams(("arbitrary",)),
        name="moe_combine",
    )(pos3, pos3, h1, gates, fg, ys3)


def _pick_tile(n, pref):
    t = pref
    while n % t:
        t //= 2
    return t


def kernel(x, rel_bias, norm1_g, w_in, conv_w, conv_b, gate_b, mlstm_norm_g, lam_q1, lam_k1, lam_q2,
           lam_k2, diff_norm_g, w_out, norm2_g, router_w, router_b, w_gu, b_gu, w_down, b_down, final_g):
    batch, seq, d = x.shape
    depth = w_in.shape[0]
    t_ = batch * seq
    nh = M_HEADS
    mw = conv_w.shape[-1] // 2
    dh = mw // nh
    aw = w_out.shape[1] - mw
    dv = aw // A_HEADS
    n_exp = router_w.shape[-1]
    dff = w_down.shape[-2]
    assert seq % ATT_BLOCK == 0 and seq % MLSTM_BLOCK == 0 and dh == LANES and dv == LANES and n_exp <= LANES
    assert d == ROW_TILE * LANES, 'token rows are moved as single (8, 128) f32 tiles'
    tm = _pick_tile(t_, 512)
    tb = MLSTM_BLOCK
    r = EXPERT_ROWS
    p_rows = t_ * TOP_K + n_exp * r

    bias_tiles = _bias_tiles(rel_bias)
    h = x.reshape(t_, d)
    for l in range(depth):
        w = w_in[l]
        c0 = 4 * mw
        c1 = c0 + 2 * nh
        zc = jnp.zeros((d, LANES - nh), F32)
        wa = jnp.concatenate([w[:, 0:2 * mw], w[:, 2 * mw:3 * mw], w[:, 3 * mw:c0],
                              w[:, c1 + aw:c1 + 2 * aw],
                              w[:, c0:c0 + nh], zc, w[:, c0 + nh:c1], zc], axis=1).astype(BF16)
        z4 = jnp.zeros((d, 8 - nh), F32)
        wt = jnp.concatenate([w[:, c1:c1 + aw] * ((dv // 2) ** -0.5 * LOG2E), w[:, c1 + 2 * aw:c1 + 3 * aw],
                              w[:, c0:c0 + nh], z4, w[:, c0 + nh:c1], z4], axis=1).T.astype(BF16)
        gb = gate_b[l].astype(F32)
        zl = jnp.zeros((LANES - nh,), F32)
        gbr = jnp.concatenate([gb[:nh], zl, gb[nh:], zl])[None, :]
        z8 = jnp.zeros((8 - nh,), F32)
        gbc = jnp.broadcast_to(jnp.concatenate([gb[:nh], z8, gb[nh:], z8])[:, None], (16, tm))
        mqk, mv, mo, ak, gc, qt, vt, gt = _inproj(h, norm1_g[l][None, :], wa, wt, gbr, gbc,
                                                  mw=mw, aw=aw, tm=tm)
        m_out = _mlstm(mqk, mv, mo, gc, gt, conv_w[l], conv_b[l][None, :],
                       mlstm_norm_g[l].reshape(nh, dh), batch=batch, seq=seq, nh=nh, dh=dh, tb=tb)
        lam_init = 0.8 - 0.6 * math.exp(-0.3 * l)
        lamv = jnp.stack([lam_q1[l], lam_k1[l], lam_q2[l], lam_k2[l]]).astype(F32)
        ngb = jnp.broadcast_to(diff_norm_g[l].astype(F32)[:, None], (dv, ATT_BLOCK))
        a_out = _attention(qt, ak, vt, bias_tiles, lamv, ngb, batch=batch, seq=seq, nh=A_HEADS, dv=dv,
                           lam_init=lam_init)
        wr = jnp.concatenate([router_w[l], jnp.zeros((d, LANES - n_exp), F32)], axis=1).astype(BF16)
        br = jnp.concatenate([router_b[l].astype(F32), jnp.full((LANES - n_exp,), NEG, F32)])[None, :]
        h1, u2c, idx, gates, rank, cnt = _router(h, m_out, a_out, w_out[l].astype(BF16), norm2_g[l][None, :],
                                                 wr, br, tm=tm)
        counts = cnt[0, :n_exp].astype(jnp.int32)
        padded = ((counts + r - 1) // r) * r
        pends = jnp.cumsum(padded)
        pstarts = pends - padded
        onehot = idx[:, :TOP_K, None] == jnp.arange(n_exp, dtype=jnp.int32)
        pos = (jnp.sum(jnp.where(onehot, pstarts, 0), axis=-1) + rank[:, :TOP_K]).reshape(-1)
        blk_start = jnp.arange(p_rows // r, dtype=jnp.int32) * r
        blk_e = jnp.minimum(jnp.sum(pends[None, :] <= blk_start[:, None], axis=1), n_exp - 1).astype(jnp.int32)
        n_used = (pends[-1:] // r).astype(jnp.int32)
        nrt = d // LANES
        xs3 = _dispatch(pstarts + counts, pends, n_used, pos, u2c.reshape(t_, nrt, LANES), p_rows,
                        tmd=_pick_tile(t_, 256))
        ys2 = _experts(blk_e, n_used, xs3.reshape(p_rows * nrt, LANES), w_gu[l].astype(F32),
                       b_gu[l][:, None, :].astype(F32), w_down[l].astype(F32), b_down[l][:, None, :].astype(F32))
        h = _combine(pos, h1, gates, final_g[None, :].astype(F32), ys2.reshape(p_rows, nrt, LANES),
                     tmc=_pick_tile(t_, 256), final=(l == depth - 1))
    return h.reshape(batch, seq, d)
```

```python
import functools
import math

import jax
import jax.numpy as jnp
from jax import lax
from jax.experimental import pallas as pl
from jax.experimental.pallas import tpu as pltpu

F32 = jnp.float32
BF16 = jnp.bfloat16
EPS = 1e-5
NEG = -1e30
LOG2E = math.log2(math.e)

CHUNK = 64
M_HEADS = 4
A_HEADS = 4
CONV_K = 4
N_BUCKETS = 32
MAX_DISTANCE = 128
TOP_K = 4
SWIGLU_LIMIT = 7.0
SWIGLU_ALPHA = 1.702

LANES = 128
ROW_TILE = 8
ATT_BLOCK = 512
MLSTM_BLOCK = 256
V_EXTRA = 16
EXPERT_ROWS = 256
VMEM_LIMIT = 56 * 1024 * 1024


def _cparams(sem):
    return pltpu.CompilerParams(dimension_semantics=sem, vmem_limit_bytes=VMEM_LIMIT)


def _dot(a, b):
    return jnp.dot(a, b, preferred_element_type=F32)


def _dot_nt(a, b):
    return lax.dot_general(a, b, (((1,), (1,)), ((), ())), preferred_element_type=F32)


def _log_sigmoid(x):
    return jnp.minimum(x, 0.0) - jnp.log(1.0 + jnp.exp(-jnp.abs(x)))


def _sigmoid(x):
    return 1.0 / (1.0 + jnp.exp(-x))


def _inproj_body(x_ref, g_ref, wa_ref, wt_ref, gbr_ref, gbc_ref,
                 mqk_ref, mv_ref, mo_ref, ak_ref, gc_ref, qt_ref, vt_ref, gt_ref,
                 *, mw, aw, tm):
    x = x_ref[...]
    u = x * lax.rsqrt(jnp.mean(x * x, axis=-1, keepdims=True) + EPS) * g_ref[...]
    ub = u.astype(BF16)

    def mm(a, b):
        return _dot(ub, wa_ref[:, a:b])

    c0 = 2 * mw
    mqk_ref[...] = mm(0, c0).astype(BF16)
    mv_ref[...] = mm(c0, c0 + mw).astype(BF16)
    mo_ref[...] = mm(c0 + mw, c0 + 2 * mw).astype(BF16)
    c1 = c0 + 2 * mw
    ak_ref[...] = mm(c1, c1 + aw).astype(BF16)
    gc_ref[...] = mm(c1 + aw, c1 + aw + 2 * LANES) + gbr_ref[...]
    t = _dot_nt(wt_ref[...], ub)
    qt_ref[...] = t[0:aw].astype(BF16)
    dv = aw // A_HEADS
    ones_rows = (lax.broadcasted_iota(jnp.int32, (V_EXTRA, ATT_BLOCK), 0) == 0).astype(BF16)
    for j in range(tm // ATT_BLOCK):
        for h in range(A_HEADS):
            r0 = h * (dv + V_EXTRA)
            vt_ref[j, r0:r0 + dv, :] = t[aw + h * dv:aw + (h + 1) * dv,
                                         j * ATT_BLOCK:(j + 1) * ATT_BLOCK].astype(BF16)
            vt_ref[j, r0 + dv:r0 + dv + V_EXTRA, :] = ones_rows
    gt_ref[...] = t[2 * aw:2 * aw + 16] + gbc_ref[...]


def _inproj(x2, g1, wa, wt, gbr, gbc, *, mw, aw, tm):
    t_, d = x2.shape
    nt = t_ // tm
    body = functools.partial(_inproj_body, mw=mw, aw=aw, tm=tm)
    row = lambda w: pl.BlockSpec((tm, w), lambda i: (i, 0))
    full = lambda a: pl.BlockSpec(a.shape, lambda i: (0,) * a.ndim)
    return pl.pallas_call(
        body,
        grid=(nt,),
        in_specs=[row(d), full(g1), full(wa), full(wt), full(gbr), full(gbc)],
        out_specs=[row(2 * mw), row(mw), row(mw), row(aw), row(2 * LANES),
                   pl.BlockSpec((aw, tm), lambda i: (0, i)),
                   pl.BlockSpec((tm // ATT_BLOCK, aw + A_HEADS * V_EXTRA, ATT_BLOCK), lambda i: (i, 0, 0)),
                   pl.BlockSpec((16, tm), lambda i: (0, i))],
        out_shape=[jax.ShapeDtypeStruct((t_, 2 * mw), BF16),
                   jax.ShapeDtypeStruct((t_, mw), BF16),
                   jax.ShapeDtypeStruct((t_, mw), BF16),
                   jax.ShapeDtypeStruct((t_, aw), BF16),
                   jax.ShapeDtypeStruct((t_, 2 * LANES), F32),
                   jax.ShapeDtypeStruct((aw, t_), BF16),
                   jax.ShapeDtypeStruct((t_ // ATT_BLOCK, aw + A_HEADS * V_EXTRA, ATT_BLOCK), BF16),
                   jax.ShapeDtypeStruct((16, t_), F32)],
        compiler_params=_cparams(("arbitrary",)),
        name="inproj",
    )(x2, g1, wa, wt, gbr, gbc)


def _segment_scan(x, pos, axis, op, identity):
    d = 1
    while d < CHUNK:
        sh = pltpu.roll(x, d, axis=axis)
        x = op(x, jnp.where(pos >= d, sh, identity))
        d *= 2
    return x


def _mlstm_body(mqk_ref, mv_ref, mo_ref, gc_ref, gt_ref, cw_ref, cb_ref, ng_ref, out_ref,
                ext_ref, qk_ref, kt_ref, gs_ref, cst_ref, mst_ref, *, tb, nh, dh):
    mw = nh * dh

    @pl.when(pl.program_id(1) == 0)
    def _():
        ext_ref[0:8, :] = jnp.zeros((8, 2 * mw), F32)
        cst_ref[...] = jnp.zeros_like(cst_ref)
        mst_ref[...] = jnp.zeros_like(mst_ref)

    ext_ref[8:8 + tb, :] = mqk_ref[...].astype(F32)
    y = cb_ref[...]
    for j in range(CONV_K):
        y = y + cw_ref[j:j + 1, :] * ext_ref[8 - (CONV_K - 1) + j:8 - (CONV_K - 1) + j + tb, :]
    ext_ref[0:8, :] = ext_ref[tb:tb + 8, :]
    act = y * _sigmoid(y)
    qk_ref[:, 0:mw] = act[:, 0:mw].astype(BF16)
    k_all = act[:, mw:2 * mw] * (dh ** -0.5)
    qk_ref[:, mw:2 * mw] = k_all.astype(BF16)
    kt_ref[...] = k_all.T

    gc = gc_ref[...]
    rpos = lax.broadcasted_iota(jnp.int32, (tb, LANES), 0) & (CHUNK - 1)
    bc = _segment_scan(_log_sigmoid(gc[:, LANES:]), rpos, 0, jnp.add, 0.0)
    a_c = gc[:, :LANES] - bc
    pm = _segment_scan(a_c, rpos, 0, jnp.maximum, NEG)
    gs_ref[0] = bc
    gs_ref[1] = a_c
    gs_ref[2] = pm
    gt = gt_ref[...]
    lpos = lax.broadcasted_iota(jnp.int32, (8, tb), 1) & (CHUNK - 1)
    a_r = gt[0:8] - _segment_scan(_log_sigmoid(gt[8:16]), lpos, 1, jnp.add, 0.0)

    tri = (lax.broadcasted_iota(jnp.int32, (CHUNK, CHUNK), 0)
           >= lax.broadcasted_iota(jnp.int32, (CHUNK, CHUNK), 1))
    ones_col = (lax.broadcasted_iota(jnp.int32, (CHUNK, dh), 1) == 0).astype(BF16)
    ones_grp = (lax.broadcasted_iota(jnp.int32, (LANES, dh), 1) == 0).astype(BF16)

    m_state = [mst_ref[h, 0:1, 0:1] for h in range(nh)]
    for c in range(tb // CHUNK):
        r0 = c * CHUNK
        for h in range(nh):
            q = qk_ref[r0:r0 + CHUNK, h * dh:(h + 1) * dh]
            k = qk_ref[r0:r0 + CHUNK, mw + h * dh:mw + (h + 1) * dh]
            vext = jnp.concatenate([mv_ref[r0:r0 + CHUNK, h * dh:(h + 1) * dh], ones_col], axis=1)
            m_prev = m_state[h]
            bc_h = gs_ref[0, r0:r0 + CHUNK, h:h + 1]
            ac_h = gs_ref[1, r0:r0 + CHUNK, h:h + 1]
            pm_h = gs_ref[2, r0:r0 + CHUNK, h:h + 1]
            big_m = jnp.maximum(pm_h, m_prev)
            e = jnp.where(tri, jnp.exp(a_r[h:h + 1, r0:r0 + CHUNK] - big_m), 0.0)
            w = (e * _dot_nt(q, k)).astype(BF16)
            cext = cst_ref[h]
            r = _dot(w, vext) + jnp.exp(m_prev - big_m) * _dot(q, cext.astype(BF16))
            num = r[:, 0:dh]
            den = r[:, dh:dh + 1]
            hh = num / jnp.maximum(jnp.abs(den), jnp.exp(-(bc_h + big_m)))
            yn = hh * lax.rsqrt(jnp.mean(hh * hh, axis=-1, keepdims=True) + EPS) * ng_ref[h:h + 1, :]
            og = _sigmoid(mo_ref[r0:r0 + CHUNK, h * dh:(h + 1) * dh].astype(F32))
            out_ref[r0:r0 + CHUNK, h * dh:(h + 1) * dh] = (og * yn).astype(BF16)
            m_last = jnp.maximum(m_prev, pm_h[CHUNK - 1:CHUNK, :])
            g0 = (r0 // LANES) * LANES
            in_chunk = (lax.broadcasted_iota(jnp.int32, (1, LANES), 1) // CHUNK) == (r0 - g0) // CHUNK
            wrow = jnp.where(in_chunk, jnp.exp(a_r[h:h + 1, g0:g0 + LANES] - m_last), 0.0)
            wkt = (kt_ref[h * dh:(h + 1) * dh, g0:g0 + LANES] * wrow).astype(BF16)
            vgrp = jnp.concatenate([mv_ref[g0:g0 + LANES, h * dh:(h + 1) * dh], ones_grp], axis=1)
            cst_ref[h] = jnp.exp(m_prev - m_last) * cext + _dot(wkt, vgrp)
            m_state[h] = bc_h[CHUNK - 1:CHUNK, :] + m_last
    for h in range(nh):
        mst_ref[h] = jnp.broadcast_to(m_state[h], mst_ref.shape[1:])


def _mlstm(mqk, mv, mo, gc, gt, cw, cb, ng, *, batch, seq, nh, dh, tb):
    mw = nh * dh
    nb = seq // tb
    body = functools.partial(_mlstm_body, tb=tb, nh=nh, dh=dh)
    row = lambda w: pl.BlockSpec((tb, w), lambda b, j: (b * nb + j, 0))
    full = lambda a: pl.BlockSpec(a.shape, lambda b, j: (0,) * a.ndim)
    return pl.pallas_call(
        body,
        grid=(batch, nb),
        in_specs=[row(2 * mw), row(mw), row(mw), row(2 * LANES),
                  pl.BlockSpec((16, tb), lambda b, j: (0, b * nb + j)),
                  full(cw), full(cb), full(ng)],
        out_specs=row(mw),
        out_shape=jax.ShapeDtypeStruct((batch * seq, mw), BF16),
        scratch_shapes=[pltpu.VMEM((tb + 8, 2 * mw), F32),
                        pltpu.VMEM((tb, 2 * mw), BF16),
                        pltpu.VMEM((mw, tb), F32),
                        pltpu.VMEM((3, tb, LANES), F32),
                        pltpu.VMEM((nh, dh, 2 * dh), F32),
                        pltpu.VMEM((nh, 8, LANES), F32)],
        compiler_params=_cparams(("arbitrary", "arbitrary")),
        name="mlstm",
    )(mqk, mv, mo, gc, gt, cw, cb, ng)


def _attn_body(qt_ref, k_ref, vt_ref, bias_ref, lam_ref, ng_ref, out_ref,
               acc_ref, ml_ref, s_ref, p_ref, al_ref, *, lam_init):
    bq = ATT_BLOCK
    qi = pl.program_id(2)
    qt = qt_ref[...]
    half = qt.shape[0] // 2
    rowi = lax.broadcasted_iota(jnp.int32, qt.shape, 0)
    zero = jnp.zeros_like(qt)
    qs = (jnp.where(rowi < half, qt, zero), jnp.where(rowi >= half, qt, zero))
    acc_ref[...] = jnp.zeros_like(acc_ref)
    ml_ref[...] = jnp.full(ml_ref.shape, NEG, F32)

    def scores(j, slot, biased):
        kb = k_ref[pl.ds(pl.multiple_of(j * bq, bq), bq), :]
        for m in range(2):
            s = _dot(kb, qs[m])
            if biased:
                s = s + bias_ref[0, jnp.clip(j - (qi - 2), 0, 2)]
            s_ref[slot, m] = s

    def softmax(slot):
        for m in range(2):
            s = s_ref[slot, m]
            m_old = ml_ref[m, 0:1, :]
            m_new = jnp.maximum(m_old, jnp.max(s, axis=0, keepdims=True))
            ml_ref[m, 0:1, :] = m_new
            p_ref[slot, m] = jnp.exp2(s - m_new).astype(BF16)
            al_ref[slot, m, 0:1, :] = jnp.exp2(m_old - m_new)

    def values(j, slot):
        vb = vt_ref[j]
        for m in range(2):
            acc_ref[m] = al_ref[slot, m, 0:1, :] * acc_ref[m] + _dot(vb, p_ref[slot, m])

    def step(i, cur, biased):
        scores(jnp.minimum(i + 1, qi), 1 - cur, biased)
        softmax(cur)
        values(i - 1, 1 - cur)

    def far_pair(t, carry):
        step(2 * t + 1, 1, False)
        step(2 * t + 2, 0, False)
        return carry

    def near_pair(t, carry):
        step(2 * t + 1, 1, True)
        step(2 * t + 2, 0, True)
        return carry

    n_blocks = qi + 1
    n_pairs = lax.shift_right_logical(n_blocks - 1, 1)
    n_far = lax.shift_right_logical(jnp.maximum(n_blocks - 4, 0), 1)
    scores(0, 0, True)
    scores(jnp.minimum(1, qi), 1, True)
    softmax(0)
    lax.fori_loop(0, n_far, far_pair, 0)
    lax.fori_loop(n_far, n_pairs, near_pair, 0)

    @pl.when((n_blocks & 1) == 0)
    def _():
        step(qi, 1, True)
        values(qi, 1)

    @pl.when((n_blocks & 1) == 1)
    def _():
        values(qi, 0)

    lv = lam_ref[...]
    lam = (jnp.exp(jnp.sum(lv[0:1] * lv[1:2], axis=-1, keepdims=True))
           - jnp.exp(jnp.sum(lv[2:3] * lv[3:4], axis=-1, keepdims=True)) + lam_init)
    dv = out_ref.shape[1]
    o = (acc_ref[0, 0:dv, :] / acc_ref[0, dv:dv + 1, :]
         - lam * (acc_ref[1, 0:dv, :] / acc_ref[1, dv:dv + 1, :]))
    yn = o * lax.rsqrt(jnp.mean(o * o, axis=0, keepdims=True) + EPS) * ng_ref[...] * (1.0 - lam_init)
    out_ref[...] = yn.T.astype(BF16)


def _attention(qt, ak, vt, bias, lamv, ngb, *, batch, seq, nh, dv, lam_init):
    bq = ATT_BLOCK
    nq = seq // bq
    body = functools.partial(_attn_body, lam_init=lam_init)
    return pl.pallas_call(
        body,
        grid=(batch, nh, nq),
        in_specs=[pl.BlockSpec((dv, bq), lambda b, h, i: (h, b * nq + i)),
                  pl.BlockSpec((seq, dv), lambda b, h, i: (b, h)),
                  pl.BlockSpec((nq, dv + V_EXTRA, bq), lambda b, h, i: (b, h, 0)),
                  pl.BlockSpec((1, 3, bq, bq), lambda b, h, i: (h, 0, 0, 0)),
                  pl.BlockSpec(lamv.shape, lambda b, h, i: (0, 0)),
                  pl.BlockSpec(ngb.shape, lambda b, h, i: (0, 0))],
        out_specs=pl.BlockSpec((bq, dv), lambda b, h, i: (b * nq + i, h)),
        out_shape=jax.ShapeDtypeStruct((batch * seq, nh * dv), BF16),
        scratch_shapes=[pltpu.VMEM((2, dv + V_EXTRA, bq), F32), pltpu.VMEM((2, 8, bq), F32),
                        pltpu.VMEM((2, 2, bq, bq), F32), pltpu.VMEM((2, 2, bq, bq), BF16),
                        pltpu.VMEM((2, 2, 8, bq), F32)],
        compiler_params=_cparams(("arbitrary", "arbitrary", "arbitrary")),
        name="diffattn",
    )(qt, ak, vt, bias, lamv, ngb)


def _rel_bucket(rel):
    nb = N_BUCKETS // 2
    max_exact = nb // 2
    ret = (rel > 0).astype(jnp.int32) * nb
    n = jnp.abs(rel)
    nf = jnp.maximum(n, 1).astype(F32)
    large = max_exact + (jnp.log(nf / max_exact) / math.log(MAX_DISTANCE / max_exact)
                         * (nb - max_exact)).astype(jnp.int32)
    large = jnp.minimum(large, nb - 1)
    return ret + jnp.where(n < max_exact, n, large)


def _bias_tiles(rel_bias):
    b = ATT_BLOCK
    span = 2 * b
    table = rel_bias.astype(F32)

    def lookup(rel):
        onehot = _rel_bucket(rel)[..., None] == jnp.arange(N_BUCKETS, dtype=jnp.int32)
        return jnp.sum(jnp.where(onehot[..., None], table, 0.0), axis=-2)

    def toeplitz(w):
        flat = jnp.tile(w, (b, 1))[:b * (span - 1)]
        return flat.reshape(b, span - 1, -1)[:, :b]

    j = jnp.arange(span, dtype=jnp.int32)
    far = lookup(jnp.array(-span, jnp.int32))
    diag = toeplitz(lookup(jnp.where(j < b, -j, span - j)) - far) * LOG2E
    prev = toeplitz(lookup(jnp.where(j < b, -(j + b), b - j)) - far) * LOG2E
    kpos = jnp.arange(b, dtype=jnp.int32)[:, None]
    qpos = jnp.arange(b, dtype=jnp.int32)[None, :]
    diag = jnp.where((kpos // CHUNK <= qpos // CHUNK)[..., None], diag, NEG)
    return jnp.transpose(jnp.stack([jnp.zeros_like(prev), prev, diag], axis=0), (3, 0, 1, 2))


def _router_body(x_ref, mo_ref, ao_ref, wo_ref, g2_ref, wr_ref, br_ref, tril_ref,
                 h1_ref, u2_ref, idx_ref, gate_ref, rank_ref, cnt_ref, run_ref, *, mw):
    @pl.when(pl.program_id(0) == 0)
    def _():
        run_ref[...] = jnp.zeros_like(run_ref)

    h1 = x_ref[...] + _dot(mo_ref[...], wo_ref[0:mw, :]) + _dot(ao_ref[...], wo_ref[mw:, :])
    h1_ref[...] = h1
    u2 = h1 * lax.rsqrt(jnp.mean(h1 * h1, axis=-1, keepdims=True) + EPS) * g2_ref[...]
    for j in range(u2.shape[1] // LANES):
        u2_ref[pl.ds(j, u2.shape[0], stride=ROW_TILE), :] = u2[:, j * LANES:(j + 1) * LANES]
    logits = _dot(u2.astype(BF16), wr_ref[...]) + br_ref[...]
    lane = lax.broadcasted_iota(jnp.int32, logits.shape, 1).astype(F32)
    l = logits
    vals, sels, idxs = [], [], []
    for _ in range(TOP_K):
        mx = jnp.max(l, axis=-1, keepdims=True)
        ik = jnp.min(jnp.where(l == mx, lane, float(LANES)), axis=-1, keepdims=True)
        sel = lane == ik
        l = jnp.where(sel, -jnp.inf, l)
        vals.append(mx)
        idxs.append(ik)
        sels.append(sel)
    es = [jnp.exp(v - vals[0]) for v in vals]
    inv = 1.0 / sum(es)
    oh = sum(s.astype(F32) for s in sels)
    tot = _dot(tril_ref[...], oh.astype(BF16)) + run_ref[...]
    ranks = [jnp.sum(jnp.where(s, tot, 0.0), axis=-1, keepdims=True) for s in sels]
    run_ref[...] = run_ref[...] + jnp.sum(oh, axis=0, keepdims=True)
    cnt_ref[...] = run_ref[...]

    def pack(cols):
        out = jnp.zeros(logits.shape, F32)
        for j, cj in enumerate(cols):
            out = jnp.where(lane == float(j), cj, out)
        return out

    idx_ref[...] = pack(idxs).astype(jnp.int32)
    gate_ref[...] = pack([e * inv for e in es])
    rank_ref[...] = pack(ranks).astype(jnp.int32)


def _router(x2, m_out, a_out, wo, g2, wr, br, *, tm):
    t_, d = x2.shape
    mw = m_out.shape[1]
    nt = t_ // tm
    tril = (jnp.arange(tm)[:, None] > jnp.arange(tm)[None, :]).astype(BF16)
    body = functools.partial(_router_body, mw=mw)
    row = lambda w: pl.BlockSpec((tm, w), lambda i: (i, 0))
    full = lambda a: pl.BlockSpec(a.shape, lambda i: (0,) * a.ndim)
    return pl.pallas_call(
        body,
        grid=(nt,),
        in_specs=[row(d), row(mw), row(a_out.shape[1]), full(wo), full(g2), full(wr), full(br), full(tril)],
        out_specs=[row(d), pl.BlockSpec((tm * d // LANES, LANES), lambda i: (i, 0)),
                   row(LANES), row(LANES), row(LANES),
                   pl.BlockSpec((1, LANES), lambda i: (0, 0))],
        out_shape=[jax.ShapeDtypeStruct((t_, d), F32),
                   jax.ShapeDtypeStruct((t_ * d // LANES, LANES), F32),
                   jax.ShapeDtypeStruct((t_, LANES), jnp.int32),
                   jax.ShapeDtypeStruct((t_, LANES), F32),
                   jax.ShapeDtypeStruct((t_, LANES), jnp.int32),
                   jax.ShapeDtypeStruct((1, LANES), F32)],
        scratch_shapes=[pltpu.VMEM((1, LANES), F32)],
        compiler_params=_cparams(("arbitrary",)),
        name="router",
    )(x2, m_out, a_out, wo, g2, wr, br, tril)


def _pair_loop(n_tok, fn):
    def body(r, c):
        for k in range(TOP_K):
            fn(r, k)
        return c
    lax.fori_loop(0, n_tok, body, 0, unroll=4)


def _dispatch_body(plo_ref, phi_ref, nu_ref, pos_ref, u2_ref, xs_ref, zero_ref, sem, zsem, *, tmd):
    @pl.when(pl.program_id(0) == 0)
    def _():
        zero_ref[...] = jnp.zeros_like(zero_ref)
        r = zero_ref.shape[0]

        def zrow(row):
            return pltpu.make_async_copy(zero_ref.at[0], xs_ref.at[row], zsem)

        def zblock(blk):
            return pltpu.make_async_copy(zero_ref, xs_ref.at[pl.ds(pl.multiple_of(blk * r, r), r)], zsem)

        def fill(row_fn, blk_fn):
            def per_expert(e, c):
                return lax.fori_loop(plo_ref[e], phi_ref[e], lambda row, c2: (row_fn(row), c2)[1], c)
            lax.fori_loop(0, plo_ref.shape[0], per_expert, 0)
            lax.fori_loop(nu_ref[0], xs_ref.shape[0] // r, lambda blk, c: (blk_fn(blk), c)[1], 0)

        fill(lambda row: zrow(row).start(), lambda blk: zblock(blk).start())
        fill(lambda row: zrow(row).wait(), lambda blk: zblock(blk).wait())

    def copy(r, k):
        return pltpu.make_async_copy(u2_ref.at[r], xs_ref.at[pos_ref[0, 0, r * TOP_K + k]], sem)

    _pair_loop(tmd, lambda r, k: copy(r, k).start(priority=k % 2))
    _pair_loop(tmd, lambda r, k: copy(r, k).wait())


def _dispatch(pad_lo, pad_hi, n_used, pos, u2c, p_rows, *, tmd):
    t_ = u2c.shape[0]
    nt = t_ // tmd
    pos3 = pos.reshape(nt, 1, tmd * TOP_K)
    grid_spec = pltpu.PrefetchScalarGridSpec(
        num_scalar_prefetch=3,
        grid=(nt,),
        in_specs=[pl.BlockSpec((1, 1, tmd * TOP_K), lambda i, lo, hi, nu: (i, 0, 0), memory_space=pltpu.SMEM),
                  pl.BlockSpec((tmd,) + u2c.shape[1:], lambda i, lo, hi, nu: (i, 0, 0))],
        out_specs=pl.BlockSpec(memory_space=pl.ANY),
        scratch_shapes=[pltpu.VMEM((EXPERT_ROWS,) + u2c.shape[1:], F32),
                        pltpu.SemaphoreType.DMA, pltpu.SemaphoreType.DMA],
    )
    return pl.pallas_call(
        functools.partial(_dispatch_body, tmd=tmd),
        grid_spec=grid_spec,
        out_shape=jax.ShapeDtypeStruct((p_rows,) + u2c.shape[1:], F32),
        compiler_params=_cparams(("arbitrary",)),
        name="moe_dispatch",
    )(pad_lo, pad_hi, n_used, pos3, u2c)


def _experts_body(be_ref, sl_ref, nu_ref, xs_ref, wgu_ref, bgu_ref, wd_ref, bd_ref, ys_ref, wgu_bf, wd_bf,
                  *, dff, r):
    j = pl.program_id(0)
    nb = be_ref.shape[0]
    nj = wd_ref.shape[2] // LANES
    jn = jnp.minimum(j, nb - 1)
    jp = jnp.maximum(j - 1, 0)

    @pl.when((j < nu_ref[0]) & ((j == 0) | (be_ref[jn] != be_ref[jp])))
    def _():
        wgu_bf[sl_ref[jn]] = wgu_ref[0].astype(BF16)
        wd_bf[sl_ref[jn]] = wd_ref[0].astype(BF16)

    @pl.when((j >= 1) & (jp < nu_ref[0]))
    def _():
        slot = sl_ref[jp]
        x = jnp.concatenate([xs_ref[pl.ds(c, r, stride=ROW_TILE), :] for c in range(nj)], axis=1)
        gu = _dot(x.astype(BF16), wgu_bf[slot]) + bgu_ref[0]
        gate = jnp.minimum(gu[:, :dff], SWIGLU_LIMIT)
        up = jnp.clip(gu[:, dff:], -SWIGLU_LIMIT, SWIGLU_LIMIT)
        act = (up + 1.0) * gate * _sigmoid(SWIGLU_ALPHA * gate)
        y = _dot(act.astype(BF16), wd_bf[slot]) + bd_ref[0]
        for c in range(nj):
            ys_ref[pl.ds(c, r, stride=ROW_TILE), :] = y[:, c * LANES:(c + 1) * LANES]

    @pl.when((j >= 1) & (jp >= nu_ref[0]))
    def _():
        ys_ref[...] = jnp.zeros_like(ys_ref)


def _experts(blk_e, blk_slot, n_used, xs2, wgu, bgu, wd, bd):
    r = EXPERT_ROWS
    dff, d = wd.shape[1], wd.shape[2]
    rt = r * d // LANES
    nb = xs2.shape[0] // rt
    cur = lambda j: jnp.minimum(j, nb - 1)
    prev = lambda j: jnp.maximum(j - 1, 0)
    grid_spec = pltpu.PrefetchScalarGridSpec(
        num_scalar_prefetch=3,
        grid=(nb + 1,),
        in_specs=[pl.BlockSpec((rt, LANES), lambda j, be, sl, nu: (prev(j), 0)),
                  pl.BlockSpec((1, d, 2 * dff), lambda j, be, sl, nu: (be[cur(j)], 0, 0)),
                  pl.BlockSpec((1, 1, 2 * dff), lambda j, be, sl, nu: (be[prev(j)], 0, 0)),
                  pl.BlockSpec((1, dff, d), lambda j, be, sl, nu: (be[cur(j)], 0, 0)),
                  pl.BlockSpec((1, 1, d), lambda j, be, sl, nu: (be[prev(j)], 0, 0))],
        out_specs=pl.BlockSpec((rt, LANES), lambda j, be, sl, nu: (prev(j), 0)),
        scratch_shapes=[pltpu.VMEM((2, d, 2 * dff), BF16), pltpu.VMEM((2, dff, d), BF16)],
    )
    return pl.pallas_call(
        functools.partial(_experts_body, dff=dff, r=r),
        grid_spec=grid_spec,
        out_shape=jax.ShapeDtypeStruct(xs2.shape, F32),
        compiler_params=_cparams(("arbitrary",)),
        name="moe_experts",
    )(blk_e, blk_slot, n_used, xs2, wgu, bgu, wd, bd)


def _combine_body(pos_ref, posn_ref, h1_ref, gate_ref, fg_ref, ys_ref, out_ref, rows_ref, sems,
                  *, tmc, final):
    i = pl.program_id(0)
    slot = i % 2

    def copy(pref, sl, r, k):
        return pltpu.make_async_copy(
            ys_ref.at[pref[0, 0, r * TOP_K + k]],
            rows_ref.at[sl, k, pl.ds(pl.multiple_of(r * ROW_TILE, ROW_TILE), ROW_TILE)],
            sems.at[sl])

    @pl.when(i == 0)
    def _():
        _pair_loop(tmc, lambda r, k: copy(pos_ref, 0, r, k).start(priority=k % 2))

    @pl.when(i + 1 < pl.num_programs(0))
    def _():
        _pair_loop(tmc, lambda r, k: copy(posn_ref, 1 - slot, r, k).start(priority=k % 2))

    _pair_loop(tmc, lambda r, k: copy(pos_ref, slot, r, k).wait())
    g = gate_ref[...]
    cols = []
    for j in range(h1_ref.shape[1] // LANES):
        hj = h1_ref[:, j * LANES:(j + 1) * LANES]
        for k in range(TOP_K):
            hj = hj + g[:, k:k + 1] * rows_ref[slot, k, pl.ds(j, tmc, stride=ROW_TILE), :]
        cols.append(hj)
    h = jnp.concatenate(cols, axis=1)
    if final:
        h = h * lax.rsqrt(jnp.mean(h * h, axis=-1, keepdims=True) + EPS) * fg_ref[...]
    out_ref[...] = h


def _combine(pos, h1, gates, fg, ys3, *, tmc, final):
    t_, d = h1.shape
    nt = t_ // tmc
    pos3 = pos.reshape(nt, 1, tmc * TOP_K)
    row = lambda w: pl.BlockSpec((tmc, w), lambda i: (i, 0))
    smem = lambda f: pl.BlockSpec((1, 1, tmc * TOP_K), f, memory_space=pltpu.SMEM)
    return pl.pallas_call(
        functools.partial(_combine_body, tmc=tmc, final=final),
        grid=(nt,),
        in_specs=[smem(lambda i: (i, 0, 0)), smem(lambda i: (jnp.minimum(i + 1, nt - 1), 0, 0)),
                  row(d), row(LANES), pl.BlockSpec(fg.shape, lambda i: (0, 0)),
                  pl.BlockSpec(memory_space=pl.ANY)],
        out_specs=row(d),
        out_shape=jax.ShapeDtypeStruct((t_, d), F32),
        scratch_shapes=[pltpu.VMEM((2, TOP_K, tmc * ROW_TILE, LANES), F32), pltpu.SemaphoreType.DMA((2,))],
        compiler_params=_cparams(("arbitrary",)),
        name="moe_combine",
    )(pos3, pos3, h1, gates, fg, ys3)


def _pick_tile(n, pref):
    t = pref
    while n % t:
        t //= 2
    return t


def kernel(x, rel_bias, norm1_g, w_in, conv_w, conv_b, gate_b, mlstm_norm_g, lam_q1, lam_k1, lam_q2,
           lam_k2, diff_norm_g, w_out, norm2_g, router_w, router_b, w_gu, b_gu, w_down, b_down, final_g):
    batch, seq, d = x.shape
    depth = w_in.shape[0]
    t_ = batch * seq
    nh = M_HEADS
    mw = conv_w.shape[-1] // 2
    dh = mw // nh
    aw = w_out.shape[1] - mw
    dv = aw // A_HEADS
    n_exp = router_w.shape[-1]
    dff = w_down.shape[-2]
    assert seq % ATT_BLOCK == 0 and seq % MLSTM_BLOCK == 0 and dh == LANES and dv == LANES and n_exp <= LANES
    assert d == ROW_TILE * LANES, 'token rows are moved as single (8, 128) f32 tiles'
    tm = _pick_tile(t_, 512)
    tb = MLSTM_BLOCK
    r = EXPERT_ROWS
    p_rows = t_ * TOP_K + n_exp * r

    bias_tiles = _bias_tiles(rel_bias)
    h = x.reshape(t_, d)
    for l in range(depth):
        w = w_in[l]
        c0 = 4 * mw
        c1 = c0 + 2 * nh
        zc = jnp.zeros((d, LANES - nh), F32)
        wa = jnp.concatenate([w[:, 0:2 * mw], w[:, 2 * mw:3 * mw], w[:, 3 * mw:c0],
                              w[:, c1 + aw:c1 + 2 * aw],
                              w[:, c0:c0 + nh], zc, w[:, c0 + nh:c1], zc], axis=1).astype(BF16)
        z4 = jnp.zeros((d, 8 - nh), F32)
        wt = jnp.concatenate([w[:, c1:c1 + aw] * ((dv // 2) ** -0.5 * LOG2E), w[:, c1 + 2 * aw:c1 + 3 * aw],
                              w[:, c0:c0 + nh], z4, w[:, c0 + nh:c1], z4], axis=1).T.astype(BF16)
        gb = gate_b[l].astype(F32)
        zl = jnp.zeros((LANES - nh,), F32)
        gbr = jnp.concatenate([gb[:nh], zl, gb[nh:], zl])[None, :]
        z8 = jnp.zeros((8 - nh,), F32)
        gbc = jnp.broadcast_to(jnp.concatenate([gb[:nh], z8, gb[nh:], z8])[:, None], (16, tm))
        mqk, mv, mo, ak, gc, qt, vt, gt = _inproj(h, norm1_g[l][None, :], wa, wt, gbr, gbc,
                                                  mw=mw, aw=aw, tm=tm)
        m_out = _mlstm(mqk, mv, mo, gc, gt, conv_w[l], conv_b[l][None, :],
                       mlstm_norm_g[l].reshape(nh, dh), batch=batch, seq=seq, nh=nh, dh=dh, tb=tb)
        lam_init = 0.8 - 0.6 * math.exp(-0.3 * l)
        lamv = jnp.stack([lam_q1[l], lam_k1[l], lam_q2[l], lam_k2[l]]).astype(F32)
        ngb = jnp.broadcast_to(diff_norm_g[l].astype(F32)[:, None], (dv, ATT_BLOCK))
        a_out = _attention(qt, ak, vt, bias_tiles, lamv, ngb, batch=batch, seq=seq, nh=A_HEADS, dv=dv,
                           lam_init=lam_init)
        wr = jnp.concatenate([router_w[l], jnp.zeros((d, LANES - n_exp), F32)], axis=1).astype(BF16)
        br = jnp.concatenate([router_b[l].astype(F32), jnp.full((LANES - n_exp,), NEG, F32)])[None, :]
        h1, u2c, idx, gates, rank, cnt = _router(h, m_out, a_out, w_out[l].astype(BF16), norm2_g[l][None, :],
                                                 wr, br, tm=tm)
        counts = cnt[0, :n_exp].astype(jnp.int32)
        padded = ((counts + r - 1) // r) * r
        pends = jnp.cumsum(padded)
        pstarts = pends - padded
        onehot = idx[:, :TOP_K, None] == jnp.arange(n_exp, dtype=jnp.int32)
        pos = (jnp.sum(jnp.where(onehot, pstarts, 0), axis=-1) + rank[:, :TOP_K]).reshape(-1)
        blk_start = jnp.arange(p_rows // r, dtype=jnp.int32) * r
        blk_e = jnp.minimum(jnp.sum(pends[None, :] <= blk_start[:, None], axis=1), n_exp - 1).astype(jnp.int32)
        n_used = (pends[-1:] // r).astype(jnp.int32)
        blk_slot = (jnp.cumsum(jnp.concatenate([jnp.zeros((1,), jnp.int32),
                                                (blk_e[1:] != blk_e[:-1]).astype(jnp.int32)])) & 1).astype(jnp.int32)
        nrt = d // LANES
        xs3 = _dispatch(pstarts + counts, pends, n_used, pos, u2c.reshape(t_, nrt, LANES), p_rows,
                        tmd=_pick_tile(t_, 256))
        ys2 = _experts(blk_e, blk_slot, n_used, xs3.reshape(p_rows * nrt, LANES), w_gu[l].astype(F32),
                       b_gu[l][:, None, :].astype(F32), w_down[l].astype(F32), b_down[l][:, None, :].astype(F32))
        h = _combine(pos, h1, gates, final_g[None, :].astype(F32), ys2.reshape(p_rows, nrt, LANES),
                     tmc=_pick_tile(t_, 256), final=(l == depth - 1))
    return h.reshape(batch, seq, d)
```
